```python
import jax, jax.numpy as jnp
from jax import lax
import numpy as np

D_MODEL = 1024
BATCH = 4
SEQ = 8192
DEPTH = 1

N_META = 16
GRID_W = 64
Q_BLOCK = 128
HEAD_DIM = 128
N_Q_HEADS = 8
N_KV_HEADS = 2
GQA_GROUP = N_Q_HEADS // N_KV_HEADS
ATTN_WIDTH = N_Q_HEADS * HEAD_DIM
KV_WIDTH = N_KV_HEADS * HEAD_DIM
POOL_WINDOWS = (2, 4, 8, 16)
N_POOL_GROUPS = len(POOL_WINDOWS)
POOL_GROUP_DIM = 128
POOL_WIDTH = N_POOL_GROUPS * POOL_GROUP_DIM
N_BRANCHES = 2
GATE_WIDTH = N_BRANCHES * D_MODEL
IN_WIDTH = ATTN_WIDTH + 2 * KV_WIDTH + POOL_WIDTH + GATE_WIDTH
D_FF = 4 * D_MODEL
ROPE_THETA = 10000.0
NORM_EPS = 1e-6

kernel_name = "hybrid_gated_gqa_axialrope_multipool_block"


def rms_norm(x, g):
    x32 = x.astype(jnp.float32)
    y = x32 * lax.rsqrt(jnp.mean(x32 * x32, axis=-1, keepdims=True) + NORM_EPS)
    return (y * g.astype(jnp.float32)).astype(x.dtype)


def axial_rope_tables(rows):
    quarter = HEAD_DIM // 4
    inv_freq = ROPE_THETA ** (-jnp.arange(quarter, dtype=jnp.float32) / quarter)
    zeros = jnp.zeros((N_META,), jnp.float32)
    row_ids = jnp.concatenate([zeros, jnp.repeat(jnp.arange(rows), GRID_W).astype(jnp.float32)])
    col_ids = jnp.concatenate([zeros, jnp.tile(jnp.arange(GRID_W), rows).astype(jnp.float32)])
    ang_r = row_ids[:, None] * inv_freq[None, :]
    ang_c = col_ids[:, None] * inv_freq[None, :]
    return jnp.cos(ang_r), jnp.sin(ang_r), jnp.cos(ang_c), jnp.sin(ang_c)


def apply_axial_rope(t, cos_r, sin_r, cos_c, sin_c):
    half, quarter = HEAD_DIM // 2, HEAD_DIM // 4
    t32 = t.astype(jnp.float32)

    def rot(xa, c, s):
        c = c[None, :, None, :]
        s = s[None, :, None, :]
        x1, x2 = xa[..., :quarter], xa[..., quarter:]
        return jnp.concatenate([x1 * c - x2 * s, x2 * c + x1 * s], axis=-1)

    out = jnp.concatenate([rot(t32[..., :half], cos_r, sin_r),
                           rot(t32[..., half:], cos_c, sin_c)], axis=-1)
    return out.astype(t.dtype)


def gqa_attention(q, k, v):
    b, l, _, dh = q.shape
    scale = 1.0 / np.sqrt(dh)
    qg = q.reshape(b, l, N_KV_HEADS, GQA_GROUP, dh).transpose(0, 2, 3, 1, 4)
    kt = k.transpose(0, 2, 1, 3)
    vt = v.transpose(0, 2, 1, 3)

    def attend(qb):
        s = jnp.einsum('bkgqd,bktd->bkgqt', qb, kt).astype(jnp.float32) * scale
        p = jax.nn.softmax(s, axis=-1)
        return jnp.einsum('bkgqt,bktd->bkgqd', p.astype(vt.dtype), vt)

    o_meta = attend(qg[:, :, :, :N_META])
    n_real = l - N_META
    nb = n_real // Q_BLOCK
    q_real = qg[:, :, :, N_META:].reshape(b, N_KV_HEADS, GQA_GROUP, nb, Q_BLOCK, dh)
    q_real = jnp.moveaxis(q_real, 3, 0)
    o_real = lax.map(attend, q_real)
    o_real = jnp.moveaxis(o_real, 0, 3).reshape(b, N_KV_HEADS, GQA_GROUP, n_real, dh)
    o = jnp.concatenate([o_meta, o_real], axis=3)
    return o.transpose(0, 3, 1, 2, 4).reshape(b, l, N_Q_HEADS * dh)


def multi_scale_pool(p):
    b, l, _ = p.shape
    t = jnp.arange(l)
    outs = []
    for g, w in enumerate(POOL_WINDOWS):
        xg = p[..., g * POOL_GROUP_DIM:(g + 1) * POOL_GROUP_DIM].astype(jnp.float32)
        c = jnp.concatenate([jnp.zeros((b, 1, POOL_GROUP_DIM), jnp.float32),
                             jnp.cumsum(xg, axis=1)], axis=1)
        lo = jnp.clip(t - w // 2, 0, l)
        hi = jnp.clip(t + w - w // 2, 0, l)
        cnt = (hi - lo).astype(jnp.float32)[None, :, None]
        mean = (c[:, hi] - c[:, lo]) / cnt
        outs.append(mean - xg)
    return jnp.concatenate(outs, axis=-1).astype(p.dtype)


def hybrid_layer(x, rope, pre_mix_g, q_norm_g, k_norm_g, w_in, w_attn_br,
                 w_pool_grp, pool_scale, w_pool_br, w_out, post_mix_g,
                 pre_mlp_g, w_mlp_in, w_mlp_out, post_mlp_g):
    b, l, _ = x.shape
    h = rms_norm(x, pre_mix_g)
    proj = h @ w_in
    s1 = ATTN_WIDTH
    s2 = s1 + KV_WIDTH
    s3 = s2 + KV_WIDTH
    s4 = s3 + POOL_WIDTH
    q, k, v, p_in, gate_logits = jnp.split(proj, [s1, s2, s3, s4], axis=-1)
    q = rms_norm(q.reshape(b, l, N_Q_HEADS, HEAD_DIM), q_norm_g)
    k = rms_norm(k.reshape(b, l, N_KV_HEADS, HEAD_DIM), k_norm_g)
    v = v.reshape(b, l, N_KV_HEADS, HEAD_DIM)
    q = apply_axial_rope(q, *rope)
    k = apply_axial_rope(k, *rope)
    attn_br = gqa_attention(q, k, v) @ w_attn_br

    pooled = multi_scale_pool(p_in).reshape(b, l, N_POOL_GROUPS, POOL_GROUP_DIM)
    pooled = jnp.einsum('blgc,gcd->blgd', pooled, w_pool_grp).reshape(b, l, POOL_WIDTH)
    pool_br = (pooled * pool_scale) @ w_pool_br

    gates = jax.nn.sigmoid(gate_logits.astype(jnp.float32)).astype(x.dtype)
    gates = gates.reshape(b, l, N_BRANCHES, D_MODEL)
    mixed = gates[:, :, 0] * attn_br + gates[:, :, 1] * pool_br
    x = x + rms_norm(mixed @ w_out, post_mix_g)

    h = rms_norm(x, pre_mlp_g)
    u = jnp.square(jax.nn.relu(h @ w_mlp_in))
    x = x + rms_norm(u @ w_mlp_out, post_mlp_g)
    return x


def setup_inputs(seed: int = 0) -> dict:
    key = jax.random.key(seed)
    ks = jax.random.split(key, 20)
    f32 = jnp.float32

    def nrm(k, shape, scale):
        return jax.random.normal(k, shape, f32) * scale

    def gain(k, shape):
        return 1.0 + 0.02 * jax.random.normal(k, shape, f32)

    return {
        "x": jax.random.normal(ks[0], (BATCH, SEQ, D_MODEL), f32),
        "meta_tokens": nrm(ks[1], (N_META, D_MODEL), 1.0),
        "pre_mix_g": gain(ks[2], (DEPTH, D_MODEL)),
        "q_norm_g": gain(ks[3], (DEPTH, HEAD_DIM)),
        "k_norm_g": gain(ks[4], (DEPTH, HEAD_DIM)),
        "w_in": nrm(ks[5], (DEPTH, D_MODEL, IN_WIDTH), D_MODEL ** -0.5),
        "w_attn_br": nrm(ks[6], (DEPTH, ATTN_WIDTH, D_MODEL), ATTN_WIDTH ** -0.5),
        "w_pool_grp": nrm(ks[7], (DEPTH, N_POOL_GROUPS, POOL_GROUP_DIM, POOL_GROUP_DIM), POOL_GROUP_DIM ** -0.5),
        "pool_scale": gain(ks[8], (DEPTH, POOL_WIDTH)),
        "w_pool_br": nrm(ks[9], (DEPTH, POOL_WIDTH, D_MODEL), POOL_WIDTH ** -0.5),
        "w_out": nrm(ks[10], (DEPTH, D_MODEL, D_MODEL), D_MODEL ** -0.5),
        "post_mix_g": gain(ks[11], (DEPTH, D_MODEL)),
        "pre_mlp_g": gain(ks[12], (DEPTH, D_MODEL)),
        "w_mlp_in": nrm(ks[13], (DEPTH, D_MODEL, D_FF), D_MODEL ** -0.5),
        "w_mlp_out": nrm(ks[14], (DEPTH, D_FF, D_MODEL), D_FF ** -0.5),
        "post_mlp_g": gain(ks[15], (DEPTH, D_MODEL)),
    }


def reference(x, meta_tokens, pre_mix_g, q_norm_g, k_norm_g, w_in, w_attn_br,
              w_pool_grp, pool_scale, w_pool_br, w_out, post_mix_g,
              pre_mlp_g, w_mlp_in, w_mlp_out, post_mlp_g):
    b, n_real, d = x.shape
    rows = n_real // GRID_W
    meta = jnp.broadcast_to(meta_tokens.astype(x.dtype)[None], (b, N_META, d))
    h = jnp.concatenate([meta, x], axis=1)
    rope = axial_rope_tables(rows)
    for i in range(DEPTH):
        h = hybrid_layer(h, rope, pre_mix_g[i], q_norm_g[i], k_norm_g[i], w_in[i],
                         w_attn_br[i], w_pool_grp[i], pool_scale[i], w_pool_br[i],
                         w_out[i], post_mix_g[i], pre_mlp_g[i], w_mlp_in[i],
                         w_mlp_out[i], post_mlp_g[i])
    return h[:, N_META:]
```

```python
import functools
import math

import jax
import jax.numpy as jnp
from jax import lax
from jax.experimental import pallas as pl
from jax.experimental.pallas import tpu as pltpu

D_MODEL = 1024
N_META = 16
GRID_W = 64
HEAD_DIM = 128
N_Q_HEADS = 8
N_KV_HEADS = 2
GQA_GROUP = N_Q_HEADS // N_KV_HEADS
ATTN_WIDTH = N_Q_HEADS * HEAD_DIM
KV_WIDTH = N_KV_HEADS * HEAD_DIM
POOL_WINDOWS = (2, 4, 8, 16)
N_POOL_GROUPS = len(POOL_WINDOWS)
POOL_GROUP_DIM = 128
POOL_WIDTH = N_POOL_GROUPS * POOL_GROUP_DIM
GATE_WIDTH = 2 * D_MODEL
D_FF = 4 * D_MODEL
ROPE_THETA = 10000.0
NORM_EPS = 1e-6

Q_OFF = 0
K_OFF = Q_OFF + ATTN_WIDTH
V_OFF = K_OFF + KV_WIDTH
P_OFF = V_OFF + KV_WIDTH
G_OFF = P_OFF + POOL_WIDTH
IN_WIDTH = G_OFF + GATE_WIDTH

V7X_VMEM_BYTES = 64 * 1024 * 1024
SUBLANES = 8
LANES = 128
META_PAD = LANES
POOL_HALO = SUBLANES

IN_TILE = 512
Q_TILE = 256
KV_CHUNK = 256
MIX_TILE = 512
MLP_TILE = 512
FF_CHUNK = 1024

Q_SCALE = math.log2(math.e) / math.sqrt(HEAD_DIM)
MASK_VALUE = -1e30

F32 = jnp.float32
BF16 = jnp.bfloat16
NT_DIMS = (((1,), (1,)), ((), ()))


def _vmem_limit(nbytes):
    return int(min(V7X_VMEM_BYTES - (4 << 20), max(nbytes, 16 << 20)))


def _const_spec(shape):
    nd = len(shape)
    return pl.BlockSpec(shape, lambda *_: (0,) * nd, pipeline_mode=pl.Buffered(1))


def _rms_scale(t):
    return lax.rsqrt(jnp.mean(t * t, axis=-1, keepdims=True) + NORM_EPS)


def _norm_rope(t, gain, cos, sin, first_half):
    y = t * _rms_scale(t) * gain
    swapped = jnp.where(first_half, pltpu.roll(y, 96, 1), pltpu.roll(y, 32, 1))
    return y * cos + swapped * sin


def _in_proj_kernel(x_ref, g_ref, w_ref, qg_ref, kg_ref, cos_ref, sin_ref,
                    q_ref, k_ref, vt_ref, p_ref, gate_ref):
    tile = x_ref.shape[0]
    x = x_ref[...]
    h = (x * _rms_scale(x) * g_ref[...]).astype(BF16)
    cos = cos_ref[...]
    sin = sin_ref[...]
    lane = lax.broadcasted_iota(jnp.int32, (tile, HEAD_DIM), 1)
    first_half = (lane & 32) == 0

    qp = jnp.dot(h, w_ref[:, Q_OFF:K_OFF], preferred_element_type=F32)
    qg = qg_ref[...] * Q_SCALE
    for hd in range(N_Q_HEADS):
        o = _norm_rope(qp[:, hd * HEAD_DIM:(hd + 1) * HEAD_DIM], qg, cos, sin, first_half)
        o = o.astype(BF16)
        grp, j = divmod(hd, GQA_GROUP)
        for s in range(tile // Q_TILE):
            q_ref[grp, s, j * Q_TILE:(j + 1) * Q_TILE, :] = o[s * Q_TILE:(s + 1) * Q_TILE]

    kp = jnp.dot(h, w_ref[:, K_OFF:V_OFF], preferred_element_type=F32)
    kg = kg_ref[...]
    for hd in range(N_KV_HEADS):
        o = _norm_rope(kp[:, hd * HEAD_DIM:(hd + 1) * HEAD_DIM], kg, cos, sin, first_half)
        k_ref[:, hd * HEAD_DIM:(hd + 1) * HEAD_DIM] = o.astype(BF16)

    vp = jnp.dot(h, w_ref[:, V_OFF:P_OFF], preferred_element_type=F32)
    for hd in range(N_KV_HEADS):
        vt = vp[:, hd * HEAD_DIM:(hd + 1) * HEAD_DIM].T.astype(BF16)
        for c in range(tile // KV_CHUNK):
            vt_ref[hd, c] = vt[:, c * KV_CHUNK:(c + 1) * KV_CHUNK]

    p_ref[...] = jnp.dot(h, w_ref[:, P_OFF:G_OFF], preferred_element_type=F32)

    gl = jnp.dot(h, w_ref[:, G_OFF:IN_WIDTH], preferred_element_type=F32)
    gate_ref[...] = jax.nn.sigmoid(gl).astype(BF16)


def _in_proj(x, pre_g, w_in, qg, kg, cos_tab, sin_tab):
    b, s, d = x.shape
    t = IN_TILE
    nt = s // t
    out_shape = (
        jax.ShapeDtypeStruct((b, N_KV_HEADS, s // Q_TILE, GQA_GROUP * Q_TILE, HEAD_DIM), BF16),
        jax.ShapeDtypeStruct((b, s, KV_WIDTH), BF16),
        jax.ShapeDtypeStruct((b, N_KV_HEADS, s // KV_CHUNK, HEAD_DIM, KV_CHUNK), BF16),
        jax.ShapeDtypeStruct((b, s, POOL_WIDTH), F32),
        jax.ShapeDtypeStruct((b, s, GATE_WIDTH), BF16),
    )
    in_specs = [
        pl.BlockSpec((None, t, d), lambda bi, i: (bi, i, 0)),
        _const_spec((1, d)),
        _const_spec((d, IN_WIDTH)),
        _const_spec((1, HEAD_DIM)),
        _const_spec((1, HEAD_DIM)),
        pl.BlockSpec((t, HEAD_DIM), lambda bi, i: (i, 0)),
        pl.BlockSpec((t, HEAD_DIM), lambda bi, i: (i, 0)),
    ]
    out_specs = (
        pl.BlockSpec((None, N_KV_HEADS, t // Q_TILE, GQA_GROUP * Q_TILE, HEAD_DIM),
                     lambda bi, i: (bi, 0, i, 0, 0)),
        pl.BlockSpec((None, t, KV_WIDTH), lambda bi, i: (bi, i, 0)),
        pl.BlockSpec((None, N_KV_HEADS, t // KV_CHUNK, HEAD_DIM, KV_CHUNK),
                     lambda bi, i: (bi, 0, i, 0, 0)),
        pl.BlockSpec((None, t, POOL_WIDTH), lambda bi, i: (bi, i, 0)),
        pl.BlockSpec((None, t, GATE_WIDTH), lambda bi, i: (bi, i, 0)),
    )
    vmem = (2 * t * d * 4 + d * IN_WIDTH * 2
            + 2 * t * (ATTN_WIDTH * 2 + 2 * KV_WIDTH * 2 + POOL_WIDTH * 4 + GATE_WIDTH * 2)
            + 4 * t * GATE_WIDTH * 4)
    return pl.pallas_call(
        _in_proj_kernel,
        grid=(b, nt),
        in_specs=in_specs,
        out_specs=out_specs,
        out_shape=out_shape,
        compiler_params=pltpu.CompilerParams(
            dimension_semantics=("arbitrary", "arbitrary"),
            vmem_limit_bytes=_vmem_limit(vmem)),
        name="in_proj",
    )(x, pre_g, w_in, qg, kg, cos_tab, sin_tab)


def _meta_proj_kernel(x_ref, g_ref, w_ref, kg_ref, k_ref, vt_ref, p_ref):
    x = x_ref[...]
    h = (x * _rms_scale(x) * g_ref[...]).astype(BF16)
    proj = jnp.dot(h, w_ref[...], preferred_element_type=F32)
    kg = kg_ref[...]
    for hd in range(N_KV_HEADS):
        t = proj[:, hd * HEAD_DIM:(hd + 1) * HEAD_DIM]
        k_ref[:, hd * HEAD_DIM:(hd + 1) * HEAD_DIM] = (t * _rms_scale(t) * kg).astype(BF16)
        v = proj[:, KV_WIDTH + hd * HEAD_DIM:KV_WIDTH + (hd + 1) * HEAD_DIM]
        vt_ref[hd] = v.T.astype(BF16)
    p_ref[...] = proj[:, 2 * KV_WIDTH:]


def _meta_proj(meta_pad, pre_g, w_in, kg):
    width = G_OFF - K_OFF
    assert width == D_MODEL and K_OFF == width
    return pl.pallas_call(
        _meta_proj_kernel,
        grid=(1,),
        in_specs=[
            pl.BlockSpec((META_PAD, D_MODEL), lambda i: (0, 0)),
            pl.BlockSpec((1, D_MODEL), lambda i: (0, 0)),
            pl.BlockSpec((D_MODEL, width), lambda i: (0, 1)),
            pl.BlockSpec((1, HEAD_DIM), lambda i: (0, 0)),
        ],
        out_specs=(
            pl.BlockSpec((META_PAD, KV_WIDTH), lambda i: (0, 0)),
            pl.BlockSpec((N_KV_HEADS, HEAD_DIM, META_PAD), lambda i: (0, 0, 0)),
            pl.BlockSpec((META_PAD, POOL_WIDTH), lambda i: (0, 0)),
        ),
        out_shape=(
            jax.ShapeDtypeStruct((META_PAD, KV_WIDTH), BF16),
            jax.ShapeDtypeStruct((N_KV_HEADS, HEAD_DIM, META_PAD), BF16),
            jax.ShapeDtypeStruct((META_PAD, POOL_WIDTH), F32),
        ),
        compiler_params=pltpu.CompilerParams(dimension_semantics=("arbitrary",)),
        name="meta_proj",
    )(meta_pad, pre_g, w_in, kg)


def _attn_kernel(q_ref, k_ref, vt_ref, km_ref, vtm_ref, o_ref, m_sc, l_sc, acc_sc):
    q = q_ref[...]
    n_chunks = k_ref.shape[0] // KV_CHUNK

    s = lax.dot_general(km_ref[...], q, NT_DIMS, preferred_element_type=F32)
    row = lax.broadcasted_iota(jnp.int32, s.shape, 0)
    s = jnp.where(row < N_META, s, MASK_VALUE)
    m0 = jnp.max(s, axis=0, keepdims=True)
    p = jnp.exp2(s - m0)
    m_sc[...] = m0
    l_sc[...] = jnp.sum(p, axis=0, keepdims=True)
    acc_sc[...] = jnp.dot(vtm_ref[...], p.astype(BF16), preferred_element_type=F32)

    def body(c, carry):
        start = pl.multiple_of(c * KV_CHUNK, KV_CHUNK)
        kc = k_ref[pl.ds(start, KV_CHUNK), :]
        s = lax.dot_general(kc, q, NT_DIMS, preferred_element_type=F32)
        m_prev = m_sc[...]
        m_new = jnp.maximum(m_prev, jnp.max(s, axis=0, keepdims=True))
        alpha = jnp.exp2(m_prev - m_new)
        p = jnp.exp2(s - m_new)
        l_sc[...] = alpha * l_sc[...] + jnp.sum(p, axis=0, keepdims=True)
        pv = jnp.dot(vt_ref[c], p.astype(BF16), preferred_element_type=F32)
        acc_sc[...] = alpha * acc_sc[...] + pv
        m_sc[...] = m_new
        return carry

    lax.fori_loop(0, n_chunks, body, 0)

    out = acc_sc[...] / l_sc[...]
    for j in range(GQA_GROUP):
        o_ref[:, j * HEAD_DIM:(j + 1) * HEAD_DIM] = (
            out[:, j * Q_TILE:(j + 1) * Q_TILE].T.astype(BF16))


def _attention(q, k, vt, k_meta, vt_meta):
    b, _, nq, n, _ = q.shape
    s = k.shape[1]
    vmem = (2 * (n * HEAD_DIM * 2 + 2 * s * HEAD_DIM * 2 + Q_TILE * GQA_GROUP * HEAD_DIM * 2)
            + HEAD_DIM * n * 4 + 6 * KV_CHUNK * n * 4)
    return pl.pallas_call(
        _attn_kernel,
        grid=(b, N_KV_HEADS, nq),
        in_specs=[
            pl.BlockSpec((None, None, None, n, HEAD_DIM), lambda bi, g, i: (bi, g, i, 0, 0)),
            pl.BlockSpec((None, s, HEAD_DIM), lambda bi, g, i: (bi, 0, g)),
            pl.BlockSpec((None, None, s // KV_CHUNK, HEAD_DIM, KV_CHUNK),
                         lambda bi, g, i: (bi, g, 0, 0, 0)),
            pl.BlockSpec((META_PAD, HEAD_DIM), lambda bi, g, i: (0, g)),
            pl.BlockSpec((None, HEAD_DIM, META_PAD), lambda bi, g, i: (g, 0, 0)),
        ],
        out_specs=pl.BlockSpec((None, Q_TILE, GQA_GROUP * HEAD_DIM), lambda bi, g, i: (bi, i, g)),
        out_shape=jax.ShapeDtypeStruct((b, s, ATTN_WIDTH), BF16),
        scratch_shapes=[
            pltpu.VMEM((1, n), F32),
            pltpu.VMEM((1, n), F32),
            pltpu.VMEM((HEAD_DIM, n), F32),
        ],
        compiler_params=pltpu.CompilerParams(
            dimension_semantics=("arbitrary", "arbitrary", "arbitrary"),
            vmem_limit_bytes=_vmem_limit(vmem)),
        name="attention",
    )(q, k, vt, k_meta, vt_meta)


def _post_mix_kernel(attn_ref, p_ref, prev_ref, next_ref, pm_ref, gate_ref, x_ref,
                     wa_ref, wg_ref, ps_ref, wp_ref, wo_ref, g_ref, o_ref, ext_sc):
    tile = x_ref.shape[0]
    i = pl.program_id(1)
    nt = pl.num_programs(1)
    seq = nt * tile

    ext_sc[0:POOL_HALO, :] = jnp.where(i == 0, pm_ref[...], prev_ref[...])
    ext_sc[POOL_HALO:POOL_HALO + tile, :] = p_ref[...]
    ext_sc[POOL_HALO + tile:, :] = jnp.where(i == nt - 1, 0.0, next_ref[...])

    pos = i * tile + lax.broadcasted_iota(jnp.int32, (tile, POOL_GROUP_DIM), 0)
    pooled = []
    for gi, w in enumerate(POOL_WINDOWS):
        cols = slice(gi * POOL_GROUP_DIM, (gi + 1) * POOL_GROUP_DIM)
        half = w // 2
        total = ext_sc[pl.ds(POOL_HALO - half, tile), cols]
        for j in range(1 - half, half):
            total = total + ext_sc[pl.ds(POOL_HALO + j, tile), cols]
        cnt = (jnp.minimum(pos + half, seq) - (pos - half)).astype(F32)
        pg = (total / cnt - p_ref[:, cols]).astype(BF16)
        pg = jnp.dot(pg, wg_ref[gi], preferred_element_type=F32)
        pooled.append((pg * ps_ref[:, cols]).astype(BF16))
    pool_br = jnp.dot(jnp.concatenate(pooled, axis=-1), wp_ref[...], preferred_element_type=F32)

    attn_br = jnp.dot(attn_ref[...], wa_ref[...], preferred_element_type=F32)
    gates = gate_ref[...]
    mixed = (gates[:, :D_MODEL].astype(F32) * attn_br
             + gates[:, D_MODEL:].astype(F32) * pool_br).astype(BF16)
    y = jnp.dot(mixed, wo_ref[...], preferred_element_type=F32)
    o_ref[...] = x_ref[...] + y * _rms_scale(y) * g_ref[...]


def _post_mix(attn, p_in, p_meta, gates, x, w_attn_br, w_pool_grp, pool_scale, w_pool_br,
              w_out, post_g):
    b, s, d = x.shape
    t = MIX_TILE
    nt = s // t
    hb = t // POOL_HALO
    last_hb = s // POOL_HALO - 1
    in_specs = [
        pl.BlockSpec((None, t, ATTN_WIDTH), lambda bi, i: (bi, i, 0)),
        pl.BlockSpec((None, t, POOL_WIDTH), lambda bi, i: (bi, i, 0)),
        pl.BlockSpec((None, POOL_HALO, POOL_WIDTH),
                     lambda bi, i: (bi, jnp.maximum(i * hb - 1, 0), 0)),
        pl.BlockSpec((None, POOL_HALO, POOL_WIDTH),
                     lambda bi, i: (bi, jnp.minimum((i + 1) * hb, last_hb), 0)),
        pl.BlockSpec((POOL_HALO, POOL_WIDTH), lambda bi, i: (N_META // POOL_HALO - 1, 0)),
        pl.BlockSpec((None, t, GATE_WIDTH), lambda bi, i: (bi, i, 0)),
        pl.BlockSpec((None, t, d), lambda bi, i: (bi, i, 0)),
        _const_spec((ATTN_WIDTH, d)),
        _const_spec((N_POOL_GROUPS, POOL_GROUP_DIM, POOL_GROUP_DIM)),
        _const_spec((1, POOL_WIDTH)),
        _const_spec((POOL_WIDTH, d)),
        _const_spec((d, d)),
        _const_spec((1, d)),
    ]
    vmem = (2 * t * (ATTN_WIDTH * 2 + POOL_WIDTH * 4 + GATE_WIDTH * 2 + 2 * d * 4)
            + (ATTN_WIDTH + POOL_WIDTH + d) * d * 2 + (t + 2 * POOL_HALO) * POOL_WIDTH * 4
            + 6 * t * d * 4)
    return pl.pallas_call(
        _post_mix_kernel,
        grid=(b, nt),
        in_specs=in_specs,
        out_specs=pl.BlockSpec((None, t, d), lambda bi, i: (bi, i, 0)),
        out_shape=jax.ShapeDtypeStruct((b, s, d), F32),
        scratch_shapes=[pltpu.VMEM((t + 2 * POOL_HALO, POOL_WIDTH), F32)],
        compiler_params=pltpu.CompilerParams(
            dimension_semantics=("arbitrary", "arbitrary"),
            vmem_limit_bytes=_vmem_limit(vmem)),
        name="post_mix",
    )(attn, p_in, p_in, p_in, p_meta, gates, x, w_attn_br, w_pool_grp, pool_scale, w_pool_br,
      w_out, post_g)


def _mlp_kernel(x_ref, g1_ref, w1_ref, w2_ref, g2_ref, o_ref):
    x = x_ref[...]
    h = (x * _rms_scale(x) * g1_ref[...]).astype(BF16)
    z = None
    for f in range(D_FF // FF_CHUNK):
        cols = slice(f * FF_CHUNK, (f + 1) * FF_CHUNK)
        u = jnp.dot(h, w1_ref[:, cols], preferred_element_type=F32)
        u = jnp.square(jnp.maximum(u, 0.0)).astype(BF16)
        zf = jnp.dot(u, w2_ref[cols, :], preferred_element_type=F32)
        z = zf if z is None else z + zf
    o_ref[...] = x + z * _rms_scale(z) * g2_ref[...]


def _mlp(x, pre_g, w1, w2, post_g):
    b, s, d = x.shape
    t = MLP_TILE
    vmem = 4 * t * d * 4 + 2 * d * D_FF * 2 + 4 * t * FF_CHUNK * 4 + 2 * t * d * 4
    return pl.pallas_call(
        _mlp_kernel,
        grid=(b, s // t),
        in_specs=[
            pl.BlockSpec((None, t, d), lambda bi, i: (bi, i, 0)),
            _const_spec((1, d)),
            _const_spec((d, D_FF)),
            _const_spec((D_FF, d)),
            _const_spec((1, d)),
        ],
        out_specs=pl.BlockSpec((None, t, d), lambda bi, i: (bi, i, 0)),
        out_shape=jax.ShapeDtypeStruct((b, s, d), F32),
        compiler_params=pltpu.CompilerParams(
            dimension_semantics=("arbitrary", "arbitrary"),
            vmem_limit_bytes=_vmem_limit(vmem)),
        name="mlp",
    )(x, pre_g, w1, w2, post_g)


def _rope_tables(seq):
    quarter = HEAD_DIM // 4
    inv_freq = ROPE_THETA ** (-jnp.arange(quarter, dtype=F32) / quarter)
    pos = jnp.arange(seq)
    ang_r = (pos // GRID_W).astype(F32)[:, None] * inv_freq[None, :]
    ang_c = (pos % GRID_W).astype(F32)[:, None] * inv_freq[None, :]
    cos = jnp.concatenate([jnp.cos(ang_r), jnp.cos(ang_r), jnp.cos(ang_c), jnp.cos(ang_c)], -1)
    sin = jnp.concatenate([-jnp.sin(ang_r), jnp.sin(ang_r), -jnp.sin(ang_c), jnp.sin(ang_c)], -1)
    return cos, sin


def kernel(x, meta_tokens, pre_mix_g, q_norm_g, k_norm_g, w_in, w_attn_br, w_pool_grp,
           pool_scale, w_pool_br, w_out, post_mix_g, pre_mlp_g, w_mlp_in, w_mlp_out, post_mlp_g):
    b, s, d = x.shape
    assert d == D_MODEL and s % max(IN_TILE, MIX_TILE, MLP_TILE) == 0
    assert IN_TILE % Q_TILE == 0 and IN_TILE % KV_CHUNK == 0 and s % GRID_W == 0
    assert pre_mix_g.shape[0] == 1

    w_in_b = w_in[0].astype(BF16)
    pre_g = pre_mix_g[0][None, :]
    qg = q_norm_g[0][None, :]
    kg = k_norm_g[0][None, :]
    cos_tab, sin_tab = _rope_tables(s)

    q, k, vt, p_in, gates = _in_proj(x, pre_g, w_in_b, qg, kg, cos_tab, sin_tab)

    meta_pad = jnp.zeros((META_PAD, d), F32).at[:N_META].set(meta_tokens.astype(F32))
    k_meta, vt_meta, p_meta = _meta_proj(meta_pad, pre_g, w_in_b, kg)

    attn = _attention(q, k, vt, k_meta, vt_meta)

    x1 = _post_mix(attn, p_in, p_meta, gates, x,
                   w_attn_br[0].astype(BF16), w_pool_grp[0].astype(BF16),
                   pool_scale[0][None, :], w_pool_br[0].astype(BF16),
                   w_out[0].astype(BF16), post_mix_g[0][None, :])

    return _mlp(x1, pre_mlp_g[0][None, :], w_mlp_in[0].astype(BF16),
                w_mlp_out[0].astype(BF16), post_mlp_g[0][None, :])
```

```python
import functools
import math

import jax
import jax.numpy as jnp
from jax import lax
from jax.experimental import pallas as pl
from jax.experimental.pallas import tpu as pltpu

D_MODEL = 1024
N_META = 16
GRID_W = 64
HEAD_DIM = 128
N_Q_HEADS = 8
N_KV_HEADS = 2
GQA_GROUP = N_Q_HEADS // N_KV_HEADS
ATTN_WIDTH = N_Q_HEADS * HEAD_DIM
KV_WIDTH = N_KV_HEADS * HEAD_DIM
POOL_WINDOWS = (2, 4, 8, 16)
N_POOL_GROUPS = len(POOL_WINDOWS)
POOL_GROUP_DIM = 128
POOL_WIDTH = N_POOL_GROUPS * POOL_GROUP_DIM
GATE_WIDTH = 2 * D_MODEL
D_FF = 4 * D_MODEL
ROPE_THETA = 10000.0
NORM_EPS = 1e-6

Q_OFF = 0
K_OFF = Q_OFF + ATTN_WIDTH
V_OFF = K_OFF + KV_WIDTH
P_OFF = V_OFF + KV_WIDTH
G_OFF = P_OFF + POOL_WIDTH
IN_WIDTH = G_OFF + GATE_WIDTH

V7X_VMEM_BYTES = 64 * 1024 * 1024
SUBLANES = 8
LANES = 128
BF16_ROWS = 16
VT_ROWS = HEAD_DIM + BF16_ROWS
POOL_HALO = SUBLANES

IN_TILE = 512
Q_TILE = 512
KV_CHUNK = 256
KV_UNROLL = 10
META_PAD = KV_CHUNK
MIX_TILE = 512
MLP_TILE = 512
FF_CHUNK = 1024

Q_SCALE = math.log2(math.e) / math.sqrt(HEAD_DIM)
MASK_VALUE = -1e30

F32 = jnp.float32
BF16 = jnp.bfloat16
NT_DIMS = (((1,), (1,)), ((), ()))


def _vmem_limit(nbytes):
    return int(min(V7X_VMEM_BYTES - (4 << 20), max(nbytes, 16 << 20)))


def _const_spec(shape):
    nd = len(shape)
    return pl.BlockSpec(shape, lambda *_: (0,) * nd, pipeline_mode=pl.Buffered(1))


def _rms_scale(t):
    return lax.rsqrt(jnp.mean(t * t, axis=-1, keepdims=True) + NORM_EPS)


def _norm_rope(t, gain, cos, sin, first_half):
    y = t * _rms_scale(t) * gain
    swapped = jnp.where(first_half, pltpu.roll(y, 96, 1), pltpu.roll(y, 32, 1))
    return y * cos + swapped * sin


def _in_proj_kernel(x_ref, g_ref, w_ref, qg_ref, kg_ref, cos_ref, sin_ref,
                    q_ref, k_ref, vt_ref, p_ref, gate_ref):
    tile = x_ref.shape[0]
    x = x_ref[...]
    h = (x * _rms_scale(x) * g_ref[...]).astype(BF16)
    cos = cos_ref[...]
    sin = sin_ref[...]
    lane = lax.broadcasted_iota(jnp.int32, (tile, HEAD_DIM), 1)
    first_half = (lane & 32) == 0

    qp = jnp.dot(h, w_ref[:, Q_OFF:K_OFF], preferred_element_type=F32)
    qg = qg_ref[...] * Q_SCALE
    for hd in range(N_Q_HEADS):
        o = _norm_rope(qp[:, hd * HEAD_DIM:(hd + 1) * HEAD_DIM], qg, cos, sin, first_half)
        o = o.astype(BF16)
        grp, j = divmod(hd, GQA_GROUP)
        for s in range(tile // Q_TILE):
            q_ref[grp, s, j * Q_TILE:(j + 1) * Q_TILE, :] = o[s * Q_TILE:(s + 1) * Q_TILE]

    kp = jnp.dot(h, w_ref[:, K_OFF:V_OFF], preferred_element_type=F32)
    kg = kg_ref[...]
    for hd in range(N_KV_HEADS):
        o = _norm_rope(kp[:, hd * HEAD_DIM:(hd + 1) * HEAD_DIM], kg, cos, sin, first_half)
        k_ref[:, hd * HEAD_DIM:(hd + 1) * HEAD_DIM] = o.astype(BF16)

    vp = jnp.dot(h, w_ref[:, V_OFF:P_OFF], preferred_element_type=F32)
    for hd in range(N_KV_HEADS):
        vt = vp[:, hd * HEAD_DIM:(hd + 1) * HEAD_DIM].T.astype(BF16)
        for c in range(tile // KV_CHUNK):
            vt_ref[hd, c, 0:HEAD_DIM, :] = vt[:, c * KV_CHUNK:(c + 1) * KV_CHUNK]
            vt_ref[hd, c, HEAD_DIM:, :] = jnp.ones((BF16_ROWS, KV_CHUNK), BF16)

    p_ref[...] = jnp.dot(h, w_ref[:, P_OFF:G_OFF], preferred_element_type=F32)

    gl = jnp.dot(h, w_ref[:, G_OFF:IN_WIDTH], preferred_element_type=F32)
    gate_ref[...] = jax.nn.sigmoid(gl).astype(BF16)


def _in_proj(x, pre_g, w_in, qg, kg, cos_tab, sin_tab):
    b, s, d = x.shape
    t = IN_TILE
    nt = s // t
    out_shape = (
        jax.ShapeDtypeStruct((b, N_KV_HEADS, s // Q_TILE, GQA_GROUP * Q_TILE, HEAD_DIM), BF16),
        jax.ShapeDtypeStruct((b, s, KV_WIDTH), BF16),
        jax.ShapeDtypeStruct((b, N_KV_HEADS, s // KV_CHUNK, VT_ROWS, KV_CHUNK), BF16),
        jax.ShapeDtypeStruct((b, s, POOL_WIDTH), F32),
        jax.ShapeDtypeStruct((b, s, GATE_WIDTH), BF16),
    )
    in_specs = [
        pl.BlockSpec((None, t, d), lambda bi, i: (bi, i, 0)),
        _const_spec((1, d)),
        _const_spec((d, IN_WIDTH)),
        _const_spec((1, HEAD_DIM)),
        _const_spec((1, HEAD_DIM)),
        pl.BlockSpec((t, HEAD_DIM), lambda bi, i: (i, 0)),
        pl.BlockSpec((t, HEAD_DIM), lambda bi, i: (i, 0)),
    ]
    out_specs = (
        pl.BlockSpec((None, N_KV_HEADS, t // Q_TILE, GQA_GROUP * Q_TILE, HEAD_DIM),
                     lambda bi, i: (bi, 0, i, 0, 0)),
        pl.BlockSpec((None, t, KV_WIDTH), lambda bi, i: (bi, i, 0)),
        pl.BlockSpec((None, N_KV_HEADS, t // KV_CHUNK, VT_ROWS, KV_CHUNK),
                     lambda bi, i: (bi, 0, i, 0, 0)),
        pl.BlockSpec((None, t, POOL_WIDTH), lambda bi, i: (bi, i, 0)),
        pl.BlockSpec((None, t, GATE_WIDTH), lambda bi, i: (bi, i, 0)),
    )
    vmem = (2 * t * d * 4 + d * IN_WIDTH * 2
            + 2 * t * (ATTN_WIDTH * 2 + 2 * KV_WIDTH * 2 + POOL_WIDTH * 4 + GATE_WIDTH * 2)
            + 4 * t * GATE_WIDTH * 4)
    return pl.pallas_call(
        _in_proj_kernel,
        grid=(b, nt),
        in_specs=in_specs,
        out_specs=out_specs,
        out_shape=out_shape,
        compiler_params=pltpu.CompilerParams(
            dimension_semantics=("arbitrary", "arbitrary"),
            vmem_limit_bytes=_vmem_limit(vmem)),
        name="in_proj",
    )(x, pre_g, w_in, qg, kg, cos_tab, sin_tab)


def _meta_proj_kernel(x_ref, g_ref, w_ref, kg_ref, k_ref, vt_ref, p_ref):
    x = x_ref[...]
    h = (x * _rms_scale(x) * g_ref[...]).astype(BF16)
    proj = jnp.dot(h, w_ref[...], preferred_element_type=F32)
    kg = kg_ref[...]
    for hd in range(N_KV_HEADS):
        t = proj[:, hd * HEAD_DIM:(hd + 1) * HEAD_DIM]
        k_ref[:, hd * HEAD_DIM:(hd + 1) * HEAD_DIM] = (t * _rms_scale(t) * kg).astype(BF16)
        v = proj[:, KV_WIDTH + hd * HEAD_DIM:KV_WIDTH + (hd + 1) * HEAD_DIM]
        vt_ref[hd, 0:HEAD_DIM, :] = v.T.astype(BF16)
        col = lax.broadcasted_iota(jnp.int32, (BF16_ROWS, META_PAD), 1)
        vt_ref[hd, HEAD_DIM:, :] = jnp.where(col < N_META, 1.0, 0.0).astype(BF16)
    p_ref[...] = proj[:, 2 * KV_WIDTH:]


def _meta_proj(meta_pad, pre_g, w_in, kg):
    width = G_OFF - K_OFF
    assert width == D_MODEL and K_OFF == width
    return pl.pallas_call(
        _meta_proj_kernel,
        grid=(1,),
        in_specs=[
            pl.BlockSpec((META_PAD, D_MODEL), lambda i: (0, 0)),
            pl.BlockSpec((1, D_MODEL), lambda i: (0, 0)),
            pl.BlockSpec((D_MODEL, width), lambda i: (0, 1)),
            pl.BlockSpec((1, HEAD_DIM), lambda i: (0, 0)),
        ],
        out_specs=(
            pl.BlockSpec((META_PAD, KV_WIDTH), lambda i: (0, 0)),
            pl.BlockSpec((N_KV_HEADS, VT_ROWS, META_PAD), lambda i: (0, 0, 0)),
            pl.BlockSpec((META_PAD, POOL_WIDTH), lambda i: (0, 0)),
        ),
        out_shape=(
            jax.ShapeDtypeStruct((META_PAD, KV_WIDTH), BF16),
            jax.ShapeDtypeStruct((N_KV_HEADS, VT_ROWS, META_PAD), BF16),
            jax.ShapeDtypeStruct((META_PAD, POOL_WIDTH), F32),
        ),
        compiler_params=pltpu.CompilerParams(dimension_semantics=("arbitrary",)),
        name="meta_proj",
    )(meta_pad, pre_g, w_in, kg)


def _attn_kernel(q_ref, k_ref, vt_ref, km_ref, vtm_ref, o_ref, s_sc, cmax_sc, m_sc, acc_sc):
    n_real = k_ref.shape[0] // KV_CHUNK

    def q_head(j):
        return q_ref[j * Q_TILE:(j + 1) * Q_TILE, :]

    def produce(j, c, slot):
        if isinstance(c, int) and c == n_real:
            kc = km_ref[...]
        elif isinstance(c, int):
            kc = k_ref[c * KV_CHUNK:(c + 1) * KV_CHUNK, :]
        else:
            kc = k_ref[pl.ds(pl.multiple_of(c * KV_CHUNK, KV_CHUNK), KV_CHUNK), :]
        s = lax.dot_general(kc, q_head(j), NT_DIMS, preferred_element_type=F32)
        if isinstance(c, int) and c == n_real:
            row = lax.broadcasted_iota(jnp.int32, s.shape, 0)
            s = jnp.where(row < N_META, s, MASK_VALUE)
        s_sc[slot, j] = s
        cmax_sc[slot, j] = jnp.max(s, axis=0, keepdims=True)

    def consume(j, c, slot):
        vt = vtm_ref[...] if isinstance(c, int) and c == n_real else vt_ref[c]
        m_prev = m_sc[j]
        m_new = jnp.maximum(m_prev, cmax_sc[slot, j])
        alpha = jnp.exp2(m_prev - m_new)
        p = jnp.exp2(s_sc[slot, j] - m_new).astype(BF16)
        pv = jnp.dot(vt, p, preferred_element_type=F32)
        acc_sc[j] = alpha * acc_sc[j] + pv
        m_sc[j] = m_new

    for j in range(GQA_GROUP):
        m_sc[j] = jnp.full((1, Q_TILE), MASK_VALUE, F32)
        acc_sc[j] = jnp.zeros((VT_ROWS, Q_TILE), F32)
        produce(j, 0, 0)

    def steps(c0, count):
        for u in range(count):
            for j in range(GQA_GROUP):
                produce(j, c0 + u + 1, (u + 1) % 2)
                consume(j, c0 + u, u % 2)

    def body(i, carry):
        steps(i * KV_UNROLL, KV_UNROLL)
        return carry

    n_loop = n_real // KV_UNROLL
    lax.fori_loop(0, n_loop, body, 0)
    done = n_loop * KV_UNROLL
    steps(done, n_real - done)
    for j in range(GQA_GROUP):
        consume(j, n_real, n_real % 2)
        acc = acc_sc[j]
        out = acc[:HEAD_DIM] / acc[HEAD_DIM:HEAD_DIM + 1]
        o_ref[:, j * HEAD_DIM:(j + 1) * HEAD_DIM] = out.T.astype(BF16)


def _attention(q, k, vt, k_meta, vt_meta):
    b, _, nq, n, _ = q.shape
    s = k.shape[1]
    vmem = (2 * (n * HEAD_DIM * 2 + s * HEAD_DIM * 2 + s * VT_ROWS * 2
                 + Q_TILE * GQA_GROUP * HEAD_DIM * 2)
            + GQA_GROUP * Q_TILE * 4 * (2 * KV_CHUNK + VT_ROWS + 3 * SUBLANES)
            + 8 * KV_CHUNK * Q_TILE * 4)
    return pl.pallas_call(
        _attn_kernel,
        grid=(b, N_KV_HEADS, nq),
        in_specs=[
            pl.BlockSpec((None, None, None, n, HEAD_DIM), lambda bi, g, i: (bi, g, i, 0, 0)),
            pl.BlockSpec((None, s, HEAD_DIM), lambda bi, g, i: (bi, 0, g)),
            pl.BlockSpec((None, None, s // KV_CHUNK, VT_ROWS, KV_CHUNK),
                         lambda bi, g, i: (bi, g, 0, 0, 0)),
            pl.BlockSpec((META_PAD, HEAD_DIM), lambda bi, g, i: (0, g)),
            pl.BlockSpec((None, VT_ROWS, META_PAD), lambda bi, g, i: (g, 0, 0)),
        ],
        out_specs=pl.BlockSpec((None, Q_TILE, GQA_GROUP * HEAD_DIM), lambda bi, g, i: (bi, i, g)),
        out_shape=jax.ShapeDtypeStruct((b, s, ATTN_WIDTH), BF16),
        scratch_shapes=[
            pltpu.VMEM((2, GQA_GROUP, KV_CHUNK, Q_TILE), F32),
            pltpu.VMEM((2, GQA_GROUP, 1, Q_TILE), F32),
            pltpu.VMEM((GQA_GROUP, 1, Q_TILE), F32),
            pltpu.VMEM((GQA_GROUP, VT_ROWS, Q_TILE), F32),
        ],
        compiler_params=pltpu.CompilerParams(
            dimension_semantics=("arbitrary", "arbitrary", "arbitrary"),
            vmem_limit_bytes=_vmem_limit(vmem)),
        name="attention",
    )(q, k, vt, k_meta, vt_meta)


def _post_mix_kernel(attn_ref, p_ref, prev_ref, next_ref, pm_ref, gate_ref, x_ref,
                     wa_ref, wg_ref, ps_ref, wp_ref, wo_ref, g_ref, o_ref, ext_sc):
    tile = x_ref.shape[0]
    i = pl.program_id(1)
    nt = pl.num_programs(1)
    seq = nt * tile

    ext_sc[0:POOL_HALO, :] = jnp.where(i == 0, pm_ref[...], prev_ref[...])
    ext_sc[POOL_HALO:POOL_HALO + tile, :] = p_ref[...]
    ext_sc[POOL_HALO + tile:, :] = jnp.where(i == nt - 1, 0.0, next_ref[...])

    pos = i * tile + lax.broadcasted_iota(jnp.int32, (tile, POOL_GROUP_DIM), 0)
    pooled = []
    for gi, w in enumerate(POOL_WINDOWS):
        cols = slice(gi * POOL_GROUP_DIM, (gi + 1) * POOL_GROUP_DIM)
        half = w // 2
        total = ext_sc[pl.ds(POOL_HALO - half, tile), cols]
        for j in range(1 - half, half):
            total = total + ext_sc[pl.ds(POOL_HALO + j, tile), cols]
        cnt = (jnp.minimum(pos + half, seq) - (pos - half)).astype(F32)
        pg = (total / cnt - p_ref[:, cols]).astype(BF16)
        pg = jnp.dot(pg, wg_ref[gi], preferred_element_type=F32)
        pooled.append((pg * ps_ref[:, cols]).astype(BF16))
    pool_br = jnp.dot(jnp.concatenate(pooled, axis=-1), wp_ref[...], preferred_element_type=F32)

    attn_br = jnp.dot(attn_ref[...], wa_ref[...], preferred_element_type=F32)
    gates = gate_ref[...]
    mixed = (gates[:, :D_MODEL].astype(F32) * attn_br
             + gates[:, D_MODEL:].astype(F32) * pool_br).astype(BF16)
    y = jnp.dot(mixed, wo_ref[...], preferred_element_type=F32)
    o_ref[...] = x_ref[...] + y * _rms_scale(y) * g_ref[...]


def _post_mix(attn, p_in, p_meta, gates, x, w_attn_br, w_pool_grp, pool_scale, w_pool_br,
              w_out, post_g):
    b, s, d = x.shape
    t = MIX_TILE
    nt = s // t
    hb = t // POOL_HALO
    last_hb = s // POOL_HALO - 1
    in_specs = [
        pl.BlockSpec((None, t, ATTN_WIDTH), lambda bi, i: (bi, i, 0)),
        pl.BlockSpec((None, t, POOL_WIDTH), lambda bi, i: (bi, i, 0)),
        pl.BlockSpec((None, POOL_HALO, POOL_WIDTH),
                     lambda bi, i: (bi, jnp.maximum(i * hb - 1, 0), 0)),
        pl.BlockSpec((None, POOL_HALO, POOL_WIDTH),
                     lambda bi, i: (bi, jnp.minimum((i + 1) * hb, last_hb), 0)),
        pl.BlockSpec((POOL_HALO, POOL_WIDTH), lambda bi, i: (N_META // POOL_HALO - 1, 0)),
        pl.BlockSpec((None, t, GATE_WIDTH), lambda bi, i: (bi, i, 0)),
        pl.BlockSpec((None, t, d), lambda bi, i: (bi, i, 0)),
        _const_spec((ATTN_WIDTH, d)),
        _const_spec((N_POOL_GROUPS, POOL_GROUP_DIM, POOL_GROUP_DIM)),
        _const_spec((1, POOL_WIDTH)),
        _const_spec((POOL_WIDTH, d)),
        _const_spec((d, d)),
        _const_spec((1, d)),
    ]
    vmem = (2 * t * (ATTN_WIDTH * 2 + POOL_WIDTH * 4 + GATE_WIDTH * 2 + 2 * d * 4)
            + (ATTN_WIDTH + POOL_WIDTH + d) * d * 2 + (t + 2 * POOL_HALO) * POOL_WIDTH * 4
            + 6 * t * d * 4)
    return pl.pallas_call(
        _post_mix_kernel,
        grid=(b, nt),
        in_specs=in_specs,
        out_specs=pl.BlockSpec((None, t, d), lambda bi, i: (bi, i, 0)),
        out_shape=jax.ShapeDtypeStruct((b, s, d), F32),
        scratch_shapes=[pltpu.VMEM((t + 2 * POOL_HALO, POOL_WIDTH), F32)],
        compiler_params=pltpu.CompilerParams(
            dimension_semantics=("arbitrary", "arbitrary"),
            vmem_limit_bytes=_vmem_limit(vmem)),
        name="post_mix",
    )(attn, p_in, p_in, p_in, p_meta, gates, x, w_attn_br, w_pool_grp, pool_scale, w_pool_br,
      w_out, post_g)


def _mlp_kernel(x_ref, g1_ref, w1_ref, w2_ref, g2_ref, o_ref):
    x = x_ref[...]
    h = (x * _rms_scale(x) * g1_ref[...]).astype(BF16)
    z = None
    for f in range(D_FF // FF_CHUNK):
        cols = slice(f * FF_CHUNK, (f + 1) * FF_CHUNK)
        u = jnp.dot(h, w1_ref[:, cols], preferred_element_type=F32)
        u = jnp.square(jnp.maximum(u, 0.0)).astype(BF16)
        zf = jnp.dot(u, w2_ref[cols, :], preferred_element_type=F32)
        z = zf if z is None else z + zf
    o_ref[...] = x + z * _rms_scale(z) * g2_ref[...]


def _mlp(x, pre_g, w1, w2, post_g):
    b, s, d = x.shape
    t = MLP_TILE
    vmem = 4 * t * d * 4 + 2 * d * D_FF * 2 + 4 * t * FF_CHUNK * 4 + 2 * t * d * 4
    return pl.pallas_call(
        _mlp_kernel,
        grid=(b, s // t),
        in_specs=[
            pl.BlockSpec((None, t, d), lambda bi, i: (bi, i, 0)),
            _const_spec((1, d)),
            _const_spec((d, D_FF)),
            _const_spec((D_FF, d)),
            _const_spec((1, d)),
        ],
        out_specs=pl.BlockSpec((None, t, d), lambda bi, i: (bi, i, 0)),
        out_shape=jax.ShapeDtypeStruct((b, s, d), F32),
        compiler_params=pltpu.CompilerParams(
            dimension_semantics=("arbitrary", "arbitrary"),
            vmem_limit_bytes=_vmem_limit(vmem)),
        name="mlp",
    )(x, pre_g, w1, w2, post_g)


def _rope_tables(seq):
    quarter = HEAD_DIM // 4
    inv_freq = ROPE_THETA ** (-jnp.arange(quarter, dtype=F32) / quarter)
    pos = jnp.arange(seq)
    ang_r = (pos // GRID_W).astype(F32)[:, None] * inv_freq[None, :]
    ang_c = (pos % GRID_W).astype(F32)[:, None] * inv_freq[None, :]
    cos = jnp.concatenate([jnp.cos(ang_r), jnp.cos(ang_r), jnp.cos(ang_c), jnp.cos(ang_c)], -1)
    sin = jnp.concatenate([-jnp.sin(ang_r), jnp.sin(ang_r), -jnp.sin(ang_c), jnp.sin(ang_c)], -1)
    return cos, sin


def kernel(x, meta_tokens, pre_mix_g, q_norm_g, k_norm_g, w_in, w_attn_br, w_pool_grp,
           pool_scale, w_pool_br, w_out, post_mix_g, pre_mlp_g, w_mlp_in, w_mlp_out, post_mlp_g):
    b, s, d = x.shape
    assert d == D_MODEL and s % max(IN_TILE, MIX_TILE, MLP_TILE) == 0
    assert IN_TILE % Q_TILE == 0 and IN_TILE % KV_CHUNK == 0 and s % GRID_W == 0
    assert pre_mix_g.shape[0] == 1

    w_in_b = w_in[0].astype(BF16)
    pre_g = pre_mix_g[0][None, :]
    qg = q_norm_g[0][None, :]
    kg = k_norm_g[0][None, :]
    cos_tab, sin_tab = _rope_tables(s)

    q, k, vt, p_in, gates = _in_proj(x, pre_g, w_in_b, qg, kg, cos_tab, sin_tab)

    meta_pad = jnp.zeros((META_PAD, d), F32).at[:N_META].set(meta_tokens.astype(F32))
    k_meta, vt_meta, p_meta = _meta_proj(meta_pad, pre_g, w_in_b, kg)

    attn = _attention(q, k, vt, k_meta, vt_meta)

    x1 = _post_mix(attn, p_in, p_meta, gates, x,
                   w_attn_br[0].astype(BF16), w_pool_grp[0].astype(BF16),
                   pool_scale[0][None, :], w_pool_br[0].astype(BF16),
                   w_out[0].astype(BF16), post_mix_g[0][None, :])

    return _mlp(x1, pre_mlp_g[0][None, :], w_mlp_in[0].astype(BF16),
                w_mlp_out[0].astype(BF16), post_mlp_g[0][None, :])
```

```python
import functools
import math

import jax
import jax.numpy as jnp
from jax import lax
from jax.experimental import pallas as pl
from jax.experimental.pallas import tpu as pltpu

D_MODEL = 1024
N_META = 16
GRID_W = 64
HEAD_DIM = 128
N_Q_HEADS = 8
N_KV_HEADS = 2
GQA_GROUP = N_Q_HEADS // N_KV_HEADS
ATTN_WIDTH = N_Q_HEADS * HEAD_DIM
KV_WIDTH = N_KV_HEADS * HEAD_DIM
POOL_WINDOWS = (2, 4, 8, 16)
N_POOL_GROUPS = len(POOL_WINDOWS)
POOL_GROUP_DIM = 128
POOL_WIDTH = N_POOL_GROUPS * POOL_GROUP_DIM
GATE_WIDTH = 2 * D_MODEL
D_FF = 4 * D_MODEL
ROPE_THETA = 10000.0
NORM_EPS = 1e-6

Q_OFF = 0
K_OFF = Q_OFF + ATTN_WIDTH
V_OFF = K_OFF + KV_WIDTH
P_OFF = V_OFF + KV_WIDTH
G_OFF = P_OFF + POOL_WIDTH
IN_WIDTH = G_OFF + GATE_WIDTH

V7X_VMEM_BYTES = 64 * 1024 * 1024
SUBLANES = 8
LANES = 128
BF16_ROWS = 16
VT_ROWS = HEAD_DIM + BF16_ROWS
POOL_HALO = SUBLANES

IN_TILE = 512
IN_SPLIT = 2
Q_TILE = 512
KV_CHUNK = 256
KV_UNROLL = 10
META_PAD = KV_CHUNK
MIX_TILE = 512
MIX_SPLIT = 2
MLP_TILE = 512
MLP_SPLIT = 2
FF_CHUNK = 1024

Q_SCALE = math.log2(math.e) / math.sqrt(HEAD_DIM)
MASK_VALUE = -1e30

F32 = jnp.float32
BF16 = jnp.bfloat16
NT_DIMS = (((1,), (1,)), ((), ()))


def _vmem_limit(nbytes):
    return int(min(V7X_VMEM_BYTES - (4 << 20), max(nbytes, 16 << 20)))


def _const_spec(shape):
    nd = len(shape)
    return pl.BlockSpec(shape, lambda *_: (0,) * nd, pipeline_mode=pl.Buffered(1))


def _rms_scale(t):
    return lax.rsqrt(jnp.mean(t * t, axis=-1, keepdims=True) + NORM_EPS)


def _norm_rope(t, gain, cos, sin, first_half):
    y = t * _rms_scale(t) * gain
    swapped = jnp.where(first_half, pltpu.roll(y, 96, 1), pltpu.roll(y, 32, 1))
    return y * cos + swapped * sin


def _in_proj_kernel(x_ref, g_ref, w_ref, qg_ref, kg_ref, cos_ref, sin_ref,
                    q_ref, k_ref, vt_ref, p_ref, gate_ref):
    tile = x_ref.shape[0]
    sub = tile // IN_SPLIT
    lane = lax.broadcasted_iota(jnp.int32, (sub, HEAD_DIM), 1)
    first_half = (lane & 32) == 0
    qg = qg_ref[...] * Q_SCALE
    kg = kg_ref[...]

    for part in range(IN_SPLIT):
        r0 = part * sub
        rows = slice(r0, r0 + sub)
        x = x_ref[rows, :]
        h = (x * _rms_scale(x) * g_ref[...]).astype(BF16)
        cos = cos_ref[rows, :]
        sin = sin_ref[rows, :]

        qp = jnp.dot(h, w_ref[:, Q_OFF:K_OFF], preferred_element_type=F32)
        for hd in range(N_Q_HEADS):
            o = _norm_rope(qp[:, hd * HEAD_DIM:(hd + 1) * HEAD_DIM], qg, cos, sin, first_half)
            grp, j = divmod(hd, GQA_GROUP)
            q_ref[grp, j, rows, :] = o.astype(BF16)

        kp = jnp.dot(h, w_ref[:, K_OFF:V_OFF], preferred_element_type=F32)
        for hd in range(N_KV_HEADS):
            o = _norm_rope(kp[:, hd * HEAD_DIM:(hd + 1) * HEAD_DIM], kg, cos, sin, first_half)
            k_ref[rows, hd * HEAD_DIM:(hd + 1) * HEAD_DIM] = o.astype(BF16)

        vp = jnp.dot(h, w_ref[:, V_OFF:P_OFF], preferred_element_type=F32)
        for hd in range(N_KV_HEADS):
            vt = vp[:, hd * HEAD_DIM:(hd + 1) * HEAD_DIM].T.astype(BF16)
            for c in range(sub // KV_CHUNK):
                chunk = r0 // KV_CHUNK + c
                vt_ref[hd, chunk, 0:HEAD_DIM, :] = vt[:, c * KV_CHUNK:(c + 1) * KV_CHUNK]
                vt_ref[hd, chunk, HEAD_DIM:, :] = jnp.ones((BF16_ROWS, KV_CHUNK), BF16)

        p_ref[rows, :] = jnp.dot(h, w_ref[:, P_OFF:G_OFF], preferred_element_type=F32)

        gl = jnp.dot(h, w_ref[:, G_OFF:IN_WIDTH], preferred_element_type=F32)
        gate_ref[rows, :] = jax.nn.sigmoid(gl).astype(BF16)


def _in_proj(x, pre_g, w_in, qg, kg, cos_tab, sin_tab):
    b, s, d = x.shape
    t = IN_TILE
    nt = s // t
    per_q = Q_TILE // t
    out_shape = (
        jax.ShapeDtypeStruct((b, N_KV_HEADS, s // Q_TILE, GQA_GROUP, Q_TILE, HEAD_DIM), BF16),
        jax.ShapeDtypeStruct((b, s, KV_WIDTH), BF16),
        jax.ShapeDtypeStruct((b, N_KV_HEADS, s // KV_CHUNK, VT_ROWS, KV_CHUNK), BF16),
        jax.ShapeDtypeStruct((b, s, POOL_WIDTH), F32),
        jax.ShapeDtypeStruct((b, s, GATE_WIDTH), BF16),
    )
    in_specs = [
        pl.BlockSpec((None, t, d), lambda bi, i: (bi, i, 0)),
        _const_spec((1, d)),
        _const_spec((d, IN_WIDTH)),
        _const_spec((1, HEAD_DIM)),
        _const_spec((1, HEAD_DIM)),
        pl.BlockSpec((t, HEAD_DIM), lambda bi, i: (i, 0)),
        pl.BlockSpec((t, HEAD_DIM), lambda bi, i: (i, 0)),
    ]
    out_specs = (
        pl.BlockSpec((None, N_KV_HEADS, None, GQA_GROUP, t, HEAD_DIM),
                     lambda bi, i: (bi, 0, i // per_q, 0, i % per_q, 0)),
        pl.BlockSpec((None, t, KV_WIDTH), lambda bi, i: (bi, i, 0)),
        pl.BlockSpec((None, N_KV_HEADS, t // KV_CHUNK, VT_ROWS, KV_CHUNK),
                     lambda bi, i: (bi, 0, i, 0, 0)),
        pl.BlockSpec((None, t, POOL_WIDTH), lambda bi, i: (bi, i, 0)),
        pl.BlockSpec((None, t, GATE_WIDTH), lambda bi, i: (bi, i, 0)),
    )
    vmem = (2 * t * d * 4 + d * IN_WIDTH * 2
            + 2 * t * (ATTN_WIDTH * 2 + 2 * KV_WIDTH * 2 + POOL_WIDTH * 4 + GATE_WIDTH * 2)
            + 4 * t * GATE_WIDTH * 4)
    return pl.pallas_call(
        _in_proj_kernel,
        grid=(b, nt),
        in_specs=in_specs,
        out_specs=out_specs,
        out_shape=out_shape,
        compiler_params=pltpu.CompilerParams(
            dimension_semantics=("arbitrary", "arbitrary"),
            vmem_limit_bytes=_vmem_limit(vmem)),
        name="in_proj",
    )(x, pre_g, w_in, qg, kg, cos_tab, sin_tab)


def _meta_proj_kernel(x_ref, g_ref, w_ref, kg_ref, k_ref, vt_ref, p_ref):
    x = x_ref[...]
    h = (x * _rms_scale(x) * g_ref[...]).astype(BF16)
    proj = jnp.dot(h, w_ref[...], preferred_element_type=F32)
    kg = kg_ref[...]
    for hd in range(N_KV_HEADS):
        t = proj[:, hd * HEAD_DIM:(hd + 1) * HEAD_DIM]
        k_ref[:, hd * HEAD_DIM:(hd + 1) * HEAD_DIM] = (t * _rms_scale(t) * kg).astype(BF16)
        v = proj[:, KV_WIDTH + hd * HEAD_DIM:KV_WIDTH + (hd + 1) * HEAD_DIM]
        vt_ref[hd, 0:HEAD_DIM, :] = v.T.astype(BF16)
        col = lax.broadcasted_iota(jnp.int32, (BF16_ROWS, META_PAD), 1)
        vt_ref[hd, HEAD_DIM:, :] = jnp.where(col < N_META, 1.0, 0.0).astype(BF16)
    p_ref[...] = proj[:, 2 * KV_WIDTH:]


def _meta_proj(meta_pad, pre_g, w_in, kg):
    width = G_OFF - K_OFF
    assert width == D_MODEL and K_OFF == width
    return pl.pallas_call(
        _meta_proj_kernel,
        grid=(1,),
        in_specs=[
            pl.BlockSpec((META_PAD, D_MODEL), lambda i: (0, 0)),
            pl.BlockSpec((1, D_MODEL), lambda i: (0, 0)),
            pl.BlockSpec((D_MODEL, width), lambda i: (0, 1)),
            pl.BlockSpec((1, HEAD_DIM), lambda i: (0, 0)),
        ],
        out_specs=(
            pl.BlockSpec((META_PAD, KV_WIDTH), lambda i: (0, 0)),
            pl.BlockSpec((N_KV_HEADS, VT_ROWS, META_PAD), lambda i: (0, 0, 0)),
            pl.BlockSpec((META_PAD, POOL_WIDTH), lambda i: (0, 0)),
        ),
        out_shape=(
            jax.ShapeDtypeStruct((META_PAD, KV_WIDTH), BF16),
            jax.ShapeDtypeStruct((N_KV_HEADS, VT_ROWS, META_PAD), BF16),
            jax.ShapeDtypeStruct((META_PAD, POOL_WIDTH), F32),
        ),
        compiler_params=pltpu.CompilerParams(dimension_semantics=("arbitrary",)),
        name="meta_proj",
    )(meta_pad, pre_g, w_in, kg)


def _attn_kernel(q_ref, k_ref, vt_ref, km_ref, vtm_ref, o_ref, s_sc, cmax_sc, m_sc, acc_sc):
    n_real = k_ref.shape[0] // KV_CHUNK

    def q_head(j):
        return q_ref[j]

    def produce(j, c, slot):
        if isinstance(c, int) and c == n_real:
            kc = km_ref[...]
        elif isinstance(c, int):
            kc = k_ref[c * KV_CHUNK:(c + 1) * KV_CHUNK, :]
        else:
            kc = k_ref[pl.ds(pl.multiple_of(c * KV_CHUNK, KV_CHUNK), KV_CHUNK), :]
        s = lax.dot_general(kc, q_head(j), NT_DIMS, preferred_element_type=F32)
        if isinstance(c, int) and c == n_real:
            row = lax.broadcasted_iota(jnp.int32, s.shape, 0)
            s = jnp.where(row < N_META, s, MASK_VALUE)
        s_sc[slot, j] = s
        cmax_sc[slot, j] = jnp.max(s, axis=0, keepdims=True)

    def consume(j, c, slot):
        vt = vtm_ref[...] if isinstance(c, int) and c == n_real else vt_ref[c]
        m_prev = m_sc[j]
        m_new = jnp.maximum(m_prev, cmax_sc[slot, j])
        alpha = jnp.exp2(m_prev - m_new)
        p = jnp.exp2(s_sc[slot, j] - m_new).astype(BF16)
        pv = jnp.dot(vt, p, preferred_element_type=F32)
        acc_sc[j] = alpha * acc_sc[j] + pv
        m_sc[j] = m_new

    for j in range(GQA_GROUP):
        m_sc[j] = jnp.full((1, Q_TILE), MASK_VALUE, F32)
        acc_sc[j] = jnp.zeros((VT_ROWS, Q_TILE), F32)
        produce(j, 0, 0)

    def steps(c0, count):
        for u in range(count):
            for j in range(GQA_GROUP):
                produce(j, c0 + u + 1, (u + 1) % 2)
                consume(j, c0 + u, u % 2)

    def body(i, carry):
        steps(i * KV_UNROLL, KV_UNROLL)
        return carry

    n_loop = n_real // KV_UNROLL
    lax.fori_loop(0, n_loop, body, 0)
    done = n_loop * KV_UNROLL
    steps(done, n_real - done)
    for j in range(GQA_GROUP):
        consume(j, n_real, n_real % 2)
        acc = acc_sc[j]
        out = acc[:HEAD_DIM] / acc[HEAD_DIM:HEAD_DIM + 1]
        o_ref[:, j * HEAD_DIM:(j + 1) * HEAD_DIM] = out.T.astype(BF16)


def _attention(q, k, vt, k_meta, vt_meta):
    b, _, nq, _, _, _ = q.shape
    s = k.shape[1]
    vmem = (2 * (s * HEAD_DIM * 2 + s * VT_ROWS * 2 + 2 * Q_TILE * GQA_GROUP * HEAD_DIM * 2)
            + GQA_GROUP * Q_TILE * 4 * (2 * KV_CHUNK + VT_ROWS + 3 * SUBLANES)
            + 8 * KV_CHUNK * Q_TILE * 4)
    return pl.pallas_call(
        _attn_kernel,
        grid=(b, N_KV_HEADS, nq),
        in_specs=[
            pl.BlockSpec((None, None, None, GQA_GROUP, Q_TILE, HEAD_DIM),
                         lambda bi, g, i: (bi, g, i, 0, 0, 0)),
            pl.BlockSpec((None, s, HEAD_DIM), lambda bi, g, i: (bi, 0, g)),
            pl.BlockSpec((None, None, s // KV_CHUNK, VT_ROWS, KV_CHUNK),
                         lambda bi, g, i: (bi, g, 0, 0, 0)),
            pl.BlockSpec((META_PAD, HEAD_DIM), lambda bi, g, i: (0, g)),
            pl.BlockSpec((None, VT_ROWS, META_PAD), lambda bi, g, i: (g, 0, 0)),
        ],
        out_specs=pl.BlockSpec((None, Q_TILE, GQA_GROUP * HEAD_DIM), lambda bi, g, i: (bi, i, g)),
        out_shape=jax.ShapeDtypeStruct((b, s, ATTN_WIDTH), BF16),
        scratch_shapes=[
            pltpu.VMEM((2, GQA_GROUP, KV_CHUNK, Q_TILE), F32),
            pltpu.VMEM((2, GQA_GROUP, 1, Q_TILE), F32),
            pltpu.VMEM((GQA_GROUP, 1, Q_TILE), F32),
            pltpu.VMEM((GQA_GROUP, VT_ROWS, Q_TILE), F32),
        ],
        compiler_params=pltpu.CompilerParams(
            dimension_semantics=("arbitrary", "arbitrary", "arbitrary"),
            vmem_limit_bytes=_vmem_limit(vmem)),
        name="attention",
    )(q, k, vt, k_meta, vt_meta)


def _post_mix_kernel(attn_ref, p_ref, prev_ref, next_ref, pm_ref, gate_ref, x_ref,
                     wa_ref, wg_ref, ps_ref, wp_ref, wo_ref, g_ref, o_ref, ext_sc):
    tile = x_ref.shape[0]
    i = pl.program_id(1)
    nt = pl.num_programs(1)
    sub = tile // MIX_SPLIT
    ext = sub + 2 * POOL_HALO

    ext_sc[0:POOL_HALO, :] = jnp.where(i == 0, pm_ref[...], prev_ref[...])
    ext_sc[POOL_HALO:POOL_HALO + tile, :] = p_ref[...]
    ext_sc[POOL_HALO + tile:, :] = jnp.where(i == nt - 1, 0.0, next_ref[...])

    def ahead(a, k):
        return pltpu.roll(a, ext - k, 0)

    tail_row = lax.broadcasted_iota(jnp.int32, (POOL_HALO, POOL_GROUP_DIM), 0)

    for part in range(MIX_SPLIT):
        rows = slice(part * sub, (part + 1) * sub)
        attn_br = jnp.dot(attn_ref[rows, :], wa_ref[...], preferred_element_type=F32)

        pooled = []
        for gi, w in enumerate(POOL_WINDOWS):
            cols = slice(gi * POOL_GROUP_DIM, (gi + 1) * POOL_GROUP_DIM)
            half = w // 2
            e = ext_sc[part * sub:part * sub + ext, cols]
            run, span = e, 1
            while span < half:
                run = run + ahead(run, span)
                span *= 2
            if half == POOL_HALO:
                total = run[0:sub] + run[POOL_HALO:POOL_HALO + sub]
            else:
                total = (run + pltpu.roll(run, half, 0))[POOL_HALO:POOL_HALO + sub]
            mean = total * (1.0 / w)
            if part == MIX_SPLIT - 1:
                over = jnp.maximum(tail_row + (half - POOL_HALO), 0)
                tail = total[sub - POOL_HALO:] / (w - over).astype(F32)
                tail = jnp.where(i == nt - 1, tail, mean[sub - POOL_HALO:])
                mean = jnp.concatenate([mean[:sub - POOL_HALO], tail], axis=0)
            pg = (mean - p_ref[rows, cols]).astype(BF16)
            pg = jnp.dot(pg, wg_ref[gi], preferred_element_type=F32)
            pooled.append((pg * ps_ref[:, cols]).astype(BF16))
        pool_br = jnp.dot(jnp.concatenate(pooled, axis=-1), wp_ref[...],
                          preferred_element_type=F32)

        gates = gate_ref[rows, :]
        mixed = (gates[:, :D_MODEL].astype(F32) * attn_br
                 + gates[:, D_MODEL:].astype(F32) * pool_br).astype(BF16)
        y = jnp.dot(mixed, wo_ref[...], preferred_element_type=F32)
        o_ref[rows, :] = x_ref[rows, :] + y * _rms_scale(y) * g_ref[...]


def _post_mix(attn, p_in, p_meta, gates, x, w_attn_br, w_pool_grp, pool_scale, w_pool_br,
              w_out, post_g):
    b, s, d = x.shape
    t = MIX_TILE
    nt = s // t
    hb = t // POOL_HALO
    last_hb = s // POOL_HALO - 1
    in_specs = [
        pl.BlockSpec((None, t, ATTN_WIDTH), lambda bi, i: (bi, i, 0)),
        pl.BlockSpec((None, t, POOL_WIDTH), lambda bi, i: (bi, i, 0)),
        pl.BlockSpec((None, POOL_HALO, POOL_WIDTH),
                     lambda bi, i: (bi, jnp.maximum(i * hb - 1, 0), 0)),
        pl.BlockSpec((None, POOL_HALO, POOL_WIDTH),
                     lambda bi, i: (bi, jnp.minimum((i + 1) * hb, last_hb), 0)),
        pl.BlockSpec((POOL_HALO, POOL_WIDTH), lambda bi, i: (N_META // POOL_HALO - 1, 0)),
        pl.BlockSpec((None, t, GATE_WIDTH), lambda bi, i: (bi, i, 0)),
        pl.BlockSpec((None, t, d), lambda bi, i: (bi, i, 0)),
        _const_spec((ATTN_WIDTH, d)),
        _const_spec((N_POOL_GROUPS, POOL_GROUP_DIM, POOL_GROUP_DIM)),
        _const_spec((1, POOL_WIDTH)),
        _const_spec((POOL_WIDTH, d)),
        _const_spec((d, d)),
        _const_spec((1, d)),
    ]
    vmem = (2 * t * (ATTN_WIDTH * 2 + POOL_WIDTH * 4 + GATE_WIDTH * 2 + 2 * d * 4)
            + (ATTN_WIDTH + POOL_WIDTH + d) * d * 2 + (t + 2 * POOL_HALO) * POOL_WIDTH * 4
            + 6 * t * d * 4)
    return pl.pallas_call(
        _post_mix_kernel,
        grid=(b, nt),
        in_specs=in_specs,
        out_specs=pl.BlockSpec((None, t, d), lambda bi, i: (bi, i, 0)),
        out_shape=jax.ShapeDtypeStruct((b, s, d), F32),
        scratch_shapes=[pltpu.VMEM((t + 2 * POOL_HALO, POOL_WIDTH), F32)],
        compiler_params=pltpu.CompilerParams(
            dimension_semantics=("arbitrary", "arbitrary"),
            vmem_limit_bytes=_vmem_limit(vmem)),
        name="post_mix",
    )(attn, p_in, p_in, p_in, p_meta, gates, x, w_attn_br, w_pool_grp, pool_scale, w_pool_br,
      w_out, post_g)


def _mlp_kernel(x_ref, g1_ref, w1_ref, w2_ref, g2_ref, o_ref):
    sub = x_ref.shape[0] // MLP_SPLIT
    for part in range(MLP_SPLIT):
        rows = slice(part * sub, (part + 1) * sub)
        x = x_ref[rows, :]
        h = (x * _rms_scale(x) * g1_ref[...]).astype(BF16)
        z = None
        for f in range(D_FF // FF_CHUNK):
            cols = slice(f * FF_CHUNK, (f + 1) * FF_CHUNK)
            u = jnp.dot(h, w1_ref[:, cols], preferred_element_type=F32)
            u = jnp.square(jnp.maximum(u, 0.0)).astype(BF16)
            zf = jnp.dot(u, w2_ref[cols, :], preferred_element_type=F32)
            z = zf if z is None else z + zf
        o_ref[rows, :] = x + z * _rms_scale(z) * g2_ref[...]


def _mlp(x, pre_g, w1, w2, post_g):
    b, s, d = x.shape
    t = MLP_TILE
    vmem = 4 * t * d * 4 + 2 * d * D_FF * 2 + 4 * t * FF_CHUNK * 4 + 2 * t * d * 4
    return pl.pallas_call(
        _mlp_kernel,
        grid=(b, s // t),
        in_specs=[
            pl.BlockSpec((None, t, d), lambda bi, i: (bi, i, 0)),
            _const_spec((1, d)),
            _const_spec((d, D_FF)),
            _const_spec((D_FF, d)),
            _const_spec((1, d)),
        ],
        out_specs=pl.BlockSpec((None, t, d), lambda bi, i: (bi, i, 0)),
        out_shape=jax.ShapeDtypeStruct((b, s, d), F32),
        compiler_params=pltpu.CompilerParams(
            dimension_semantics=("arbitrary", "arbitrary"),
            vmem_limit_bytes=_vmem_limit(vmem)),
        name="mlp",
    )(x, pre_g, w1, w2, post_g)


def _rope_tables(seq):
    quarter = HEAD_DIM // 4
    rows = seq // GRID_W
    inv_freq = ROPE_THETA ** (-jnp.arange(quarter, dtype=F32) / quarter)
    ang_r = jnp.arange(rows, dtype=F32)[:, None] * inv_freq[None, :]
    ang_c = jnp.arange(GRID_W, dtype=F32)[:, None] * inv_freq[None, :]

    def expand(row_tab, col_tab):
        r = jnp.broadcast_to(row_tab[:, None, :], (rows, GRID_W, quarter))
        c = jnp.broadcast_to(col_tab[None, :, :], (rows, GRID_W, quarter))
        return r.reshape(seq, quarter), c.reshape(seq, quarter)

    cos_r, cos_c = expand(jnp.cos(ang_r), jnp.cos(ang_c))
    sin_r, sin_c = expand(jnp.sin(ang_r), jnp.sin(ang_c))
    cos = jnp.concatenate([cos_r, cos_r, cos_c, cos_c], -1)
    sin = jnp.concatenate([-sin_r, sin_r, -sin_c, sin_c], -1)
    return cos, sin


def kernel(x, meta_tokens, pre_mix_g, q_norm_g, k_norm_g, w_in, w_attn_br, w_pool_grp,
           pool_scale, w_pool_br, w_out, post_mix_g, pre_mlp_g, w_mlp_in, w_mlp_out, post_mlp_g):
    b, s, d = x.shape
    assert d == D_MODEL and s % max(IN_TILE, MIX_TILE, MLP_TILE) == 0
    assert Q_TILE % IN_TILE == 0 and (IN_TILE // IN_SPLIT) % KV_CHUNK == 0 and s % GRID_W == 0
    assert s % Q_TILE == 0 and KV_UNROLL % 2 == 0 and MIX_TILE % (MIX_SPLIT * BF16_ROWS) == 0
    assert pre_mix_g.shape[0] == 1

    w_in_b = w_in[0].astype(BF16)
    pre_g = pre_mix_g[0][None, :]
    qg = q_norm_g[0][None, :]
    kg = k_norm_g[0][None, :]
    cos_tab, sin_tab = _rope_tables(s)

    q, k, vt, p_in, gates = _in_proj(x, pre_g, w_in_b, qg, kg, cos_tab, sin_tab)

    meta_pad = jnp.zeros((META_PAD, d), F32).at[:N_META].set(meta_tokens.astype(F32))
    k_meta, vt_meta, p_meta = _meta_proj(meta_pad, pre_g, w_in_b, kg)

    attn = _attention(q, k, vt, k_meta, vt_meta)

    x1 = _post_mix(attn, p_in, p_meta, gates, x,
                   w_attn_br[0].astype(BF16), w_pool_grp[0].astype(BF16),
                   pool_scale[0][None, :], w_pool_br[0].astype(BF16),
                   w_out[0].astype(BF16), post_mix_g[0][None, :])

    return _mlp(x1, pre_mlp_g[0][None, :], w_mlp_in[0].astype(BF16),
                w_mlp_out[0].astype(BF16), post_mlp_g[0][None, :])
```

```python
import functools
import math

import jax
import jax.numpy as jnp
from jax import lax
from jax.experimental import pallas as pl
from jax.experimental.pallas import tpu as pltpu

D_MODEL = 1024
N_META = 16
GRID_W = 64
HEAD_DIM = 128
N_Q_HEADS = 8
N_KV_HEADS = 2
GQA_GROUP = N_Q_HEADS // N_KV_HEADS
ATTN_WIDTH = N_Q_HEADS * HEAD_DIM
KV_WIDTH = N_KV_HEADS * HEAD_DIM
POOL_WINDOWS = (2, 4, 8, 16)
N_POOL_GROUPS = len(POOL_WINDOWS)
POOL_GROUP_DIM = 128
POOL_WIDTH = N_POOL_GROUPS * POOL_GROUP_DIM
GATE_WIDTH = 2 * D_MODEL
D_FF = 4 * D_MODEL
ROPE_THETA = 10000.0
NORM_EPS = 1e-6

Q_OFF = 0
K_OFF = Q_OFF + ATTN_WIDTH
V_OFF = K_OFF + KV_WIDTH
P_OFF = V_OFF + KV_WIDTH
G_OFF = P_OFF + POOL_WIDTH
IN_WIDTH = G_OFF + GATE_WIDTH

V7X_VMEM_BYTES = 64 * 1024 * 1024
SUBLANES = 8
LANES = 128
BF16_ROWS = 16
VT_ROWS = HEAD_DIM + BF16_ROWS
POOL_HALO = SUBLANES

IN_TILE = 1024
IN_SPLIT = 4
Q_TILE = 512
KV_CHUNK = 256
KV_UNROLL = 10
META_PAD = KV_CHUNK
MIX_TILE = 1024
MIX_SPLIT = 4
MLP_TILE = 1024
MLP_SPLIT = 4
FF_CHUNK = 1024

Q_SCALE = math.log2(math.e) / math.sqrt(HEAD_DIM)
MASK_VALUE = -1e30
SCORE_BOUND = 60.0

F32 = jnp.float32
BF16 = jnp.bfloat16
NT_DIMS = (((1,), (1,)), ((), ()))


def _vmem_limit(nbytes):
    return int(min(V7X_VMEM_BYTES - (4 << 20), max(nbytes, 16 << 20)))


def _const_spec(shape):
    nd = len(shape)
    return pl.BlockSpec(shape, lambda *_: (0,) * nd, pipeline_mode=pl.Buffered(1))


def _rms_scale(t):
    return lax.rsqrt(jnp.mean(t * t, axis=-1, keepdims=True) + NORM_EPS)


def _norm_rope(t, gain, cos, sin, first_half):
    y = t * _rms_scale(t) * gain
    swapped = jnp.where(first_half, pltpu.roll(y, 96, 1), pltpu.roll(y, 32, 1))
    return y * cos + swapped * sin


def _in_proj_kernel(x_ref, g_ref, w_ref, qg_ref, kg_ref, cos_ref, sin_ref,
                    q_ref, k_ref, vt_ref, p_ref, gate_ref):
    tile = x_ref.shape[0]
    sub = tile // IN_SPLIT
    lane = lax.broadcasted_iota(jnp.int32, (sub, HEAD_DIM), 1)
    first_half = (lane & 32) == 0
    qg = qg_ref[...] * Q_SCALE
    kg = kg_ref[...]

    for part in range(IN_SPLIT):
        r0 = part * sub
        rows = slice(r0, r0 + sub)
        x = x_ref[rows, :]
        h = (x * _rms_scale(x) * g_ref[...]).astype(BF16)
        cos = cos_ref[rows, :]
        sin = sin_ref[rows, :]

        qp = jnp.dot(h, w_ref[:, Q_OFF:K_OFF], preferred_element_type=F32)
        qt, qr = divmod(r0, Q_TILE)
        for hd in range(N_Q_HEADS):
            o = _norm_rope(qp[:, hd * HEAD_DIM:(hd + 1) * HEAD_DIM], qg, cos, sin, first_half)
            grp, j = divmod(hd, GQA_GROUP)
            q_ref[grp, qt, j, qr:qr + sub, :] = o.astype(BF16)

        kp = jnp.dot(h, w_ref[:, K_OFF:V_OFF], preferred_element_type=F32)
        for hd in range(N_KV_HEADS):
            o = _norm_rope(kp[:, hd * HEAD_DIM:(hd + 1) * HEAD_DIM], kg, cos, sin, first_half)
            k_ref[rows, hd * HEAD_DIM:(hd + 1) * HEAD_DIM] = o.astype(BF16)

        vp = jnp.dot(h, w_ref[:, V_OFF:P_OFF], preferred_element_type=F32)
        for hd in range(N_KV_HEADS):
            vt = vp[:, hd * HEAD_DIM:(hd + 1) * HEAD_DIM].T.astype(BF16)
            for c in range(sub // KV_CHUNK):
                chunk = r0 // KV_CHUNK + c
                vt_ref[hd, chunk, 0:HEAD_DIM, :] = vt[:, c * KV_CHUNK:(c + 1) * KV_CHUNK]
                vt_ref[hd, chunk, HEAD_DIM:, :] = jnp.ones((BF16_ROWS, KV_CHUNK), BF16)

        p_ref[rows, :] = jnp.dot(h, w_ref[:, P_OFF:G_OFF], preferred_element_type=F32)

        gl = jnp.dot(h, w_ref[:, G_OFF:IN_WIDTH], preferred_element_type=F32)
        gate_ref[rows, :] = jax.nn.sigmoid(gl).astype(BF16)


def _in_proj(x, pre_g, w_in, qg, kg, cos_tab, sin_tab):
    b, s, d = x.shape
    t = IN_TILE
    nt = s // t
    q_tiles = max(1, t // Q_TILE)
    q_rows = min(t, Q_TILE)
    out_shape = (
        jax.ShapeDtypeStruct((b, N_KV_HEADS, s // Q_TILE, GQA_GROUP, Q_TILE, HEAD_DIM), BF16),
        jax.ShapeDtypeStruct((b, s, KV_WIDTH), BF16),
        jax.ShapeDtypeStruct((b, N_KV_HEADS, s // KV_CHUNK, VT_ROWS, KV_CHUNK), BF16),
        jax.ShapeDtypeStruct((b, s, POOL_WIDTH), F32),
        jax.ShapeDtypeStruct((b, s, GATE_WIDTH), BF16),
    )
    in_specs = [
        pl.BlockSpec((None, t, d), lambda bi, i: (bi, i, 0)),
        _const_spec((1, d)),
        _const_spec((d, IN_WIDTH)),
        _const_spec((1, HEAD_DIM)),
        _const_spec((1, HEAD_DIM)),
        pl.BlockSpec((t, HEAD_DIM), lambda bi, i: (i, 0)),
        pl.BlockSpec((t, HEAD_DIM), lambda bi, i: (i, 0)),
    ]
    out_specs = (
        pl.BlockSpec((None, N_KV_HEADS, q_tiles, GQA_GROUP, q_rows, HEAD_DIM),
                     lambda bi, i: (bi, 0, (i * t) // (q_tiles * Q_TILE), 0,
                                    ((i * t) % Q_TILE) // q_rows, 0)),
        pl.BlockSpec((None, t, KV_WIDTH), lambda bi, i: (bi, i, 0)),
        pl.BlockSpec((None, N_KV_HEADS, t // KV_CHUNK, VT_ROWS, KV_CHUNK),
                     lambda bi, i: (bi, 0, i, 0, 0)),
        pl.BlockSpec((None, t, POOL_WIDTH), lambda bi, i: (bi, i, 0)),
        pl.BlockSpec((None, t, GATE_WIDTH), lambda bi, i: (bi, i, 0)),
    )
    vmem = (2 * t * d * 4 + d * IN_WIDTH * 2
            + 2 * t * (ATTN_WIDTH * 2 + 2 * KV_WIDTH * 2 + POOL_WIDTH * 4 + GATE_WIDTH * 2)
            + 4 * (t // IN_SPLIT) * GATE_WIDTH * 4)
    return pl.pallas_call(
        _in_proj_kernel,
        grid=(b, nt),
        in_specs=in_specs,
        out_specs=out_specs,
        out_shape=out_shape,
        compiler_params=pltpu.CompilerParams(
            dimension_semantics=("arbitrary", "arbitrary"),
            vmem_limit_bytes=_vmem_limit(vmem)),
        name="in_proj",
    )(x, pre_g, w_in, qg, kg, cos_tab, sin_tab)


def _meta_proj_kernel(x_ref, g_ref, w_ref, kg_ref, k_ref, vt_ref, p_ref):
    x = x_ref[...]
    h = (x * _rms_scale(x) * g_ref[...]).astype(BF16)
    proj = jnp.dot(h, w_ref[...], preferred_element_type=F32)
    kg = kg_ref[...]
    for hd in range(N_KV_HEADS):
        t = proj[:, hd * HEAD_DIM:(hd + 1) * HEAD_DIM]
        k_ref[:, hd * HEAD_DIM:(hd + 1) * HEAD_DIM] = (t * _rms_scale(t) * kg).astype(BF16)
        v = proj[:, KV_WIDTH + hd * HEAD_DIM:KV_WIDTH + (hd + 1) * HEAD_DIM]
        vt_ref[hd, 0:HEAD_DIM, :] = v.T.astype(BF16)
        col = lax.broadcasted_iota(jnp.int32, (BF16_ROWS, META_PAD), 1)
        vt_ref[hd, HEAD_DIM:, :] = jnp.where(col < N_META, 1.0, 0.0).astype(BF16)
    p_ref[...] = proj[:, 2 * KV_WIDTH:]


def _meta_proj(meta_pad, pre_g, w_in, kg):
    width = G_OFF - K_OFF
    assert width == D_MODEL and K_OFF == width
    return pl.pallas_call(
        _meta_proj_kernel,
        grid=(1,),
        in_specs=[
            pl.BlockSpec((META_PAD, D_MODEL), lambda i: (0, 0)),
            pl.BlockSpec((1, D_MODEL), lambda i: (0, 0)),
            pl.BlockSpec((D_MODEL, width), lambda i: (0, 1)),
            pl.BlockSpec((1, HEAD_DIM), lambda i: (0, 0)),
        ],
        out_specs=(
            pl.BlockSpec((META_PAD, KV_WIDTH), lambda i: (0, 0)),
            pl.BlockSpec((N_KV_HEADS, VT_ROWS, META_PAD), lambda i: (0, 0, 0)),
            pl.BlockSpec((META_PAD, POOL_WIDTH), lambda i: (0, 0)),
        ),
        out_shape=(
            jax.ShapeDtypeStruct((META_PAD, KV_WIDTH), BF16),
            jax.ShapeDtypeStruct((N_KV_HEADS, VT_ROWS, META_PAD), BF16),
            jax.ShapeDtypeStruct((META_PAD, POOL_WIDTH), F32),
        ),
        compiler_params=pltpu.CompilerParams(dimension_semantics=("arbitrary",)),
        name="meta_proj",
    )(meta_pad, pre_g, w_in, kg)


def _attn_kernel(q_ref, k_ref, vt_ref, km_ref, vtm_ref, o_ref, s_sc, cmax_sc, m_sc, acc_sc, *,
                 bounded):
    n_real = k_ref.shape[0] // KV_CHUNK

    def q_head(j):
        return q_ref[j]

    def produce(j, c, slot):
        if isinstance(c, int) and c == n_real:
            kc = km_ref[...]
        elif isinstance(c, int):
            kc = k_ref[c * KV_CHUNK:(c + 1) * KV_CHUNK, :]
        else:
            kc = k_ref[pl.ds(pl.multiple_of(c * KV_CHUNK, KV_CHUNK), KV_CHUNK), :]
        s = lax.dot_general(kc, q_head(j), NT_DIMS, preferred_element_type=F32)
        if isinstance(c, int) and c == n_real:
            row = lax.broadcasted_iota(jnp.int32, s.shape, 0)
            s = jnp.where(row < N_META, s, MASK_VALUE)
        s_sc[slot, j] = s
        if not bounded:
            cmax_sc[slot, j] = jnp.max(s, axis=0, keepdims=True)

    def consume(j, c, slot):
        vt = vtm_ref[...] if isinstance(c, int) and c == n_real else vt_ref[c]
        if bounded:
            p = jnp.exp2(s_sc[slot, j]).astype(BF16)
            acc_sc[j] += jnp.dot(vt, p, preferred_element_type=F32)
            return
        m_prev = m_sc[j]
        m_new = jnp.maximum(m_prev, cmax_sc[slot, j])
        alpha = jnp.exp2(m_prev - m_new)
        p = jnp.exp2(s_sc[slot, j] - m_new).astype(BF16)
        pv = jnp.dot(vt, p, preferred_element_type=F32)
        acc_sc[j] = alpha * acc_sc[j] + pv
        m_sc[j] = m_new

    for j in range(GQA_GROUP):
        if not bounded:
            m_sc[j] = jnp.full((1, Q_TILE), MASK_VALUE, F32)
        acc_sc[j] = jnp.zeros((VT_ROWS, Q_TILE), F32)
        produce(j, 0, 0)

    def steps(c0, count):
        for u in range(count):
            for j in range(GQA_GROUP):
                produce(j, c0 + u + 1, (u + 1) % 2)
                consume(j, c0 + u, u % 2)

    def body(i, carry):
        steps(i * KV_UNROLL, KV_UNROLL)
        return carry

    n_loop = n_real // KV_UNROLL
    lax.fori_loop(0, n_loop, body, 0)
    done = n_loop * KV_UNROLL
    steps(done, n_real - done)
    for j in range(GQA_GROUP):
        consume(j, n_real, n_real % 2)
        acc = acc_sc[j]
        out = acc[:HEAD_DIM] / acc[HEAD_DIM:HEAD_DIM + 1]
        o_ref[:, j * HEAD_DIM:(j + 1) * HEAD_DIM] = out.T.astype(BF16)


def _attention(q, k, vt, k_meta, vt_meta, *, bounded):
    b, _, nq, _, _, _ = q.shape
    s = k.shape[1]
    vmem = (2 * (s * HEAD_DIM * 2 + s * VT_ROWS * 2 + 2 * Q_TILE * GQA_GROUP * HEAD_DIM * 2)
            + GQA_GROUP * Q_TILE * 4 * (2 * KV_CHUNK + VT_ROWS + 3 * SUBLANES)
            + 8 * KV_CHUNK * Q_TILE * 4)
    return pl.pallas_call(
        functools.partial(_attn_kernel, bounded=bounded),
        grid=(b, N_KV_HEADS, nq),
        in_specs=[
            pl.BlockSpec((None, None, None, GQA_GROUP, Q_TILE, HEAD_DIM),
                         lambda bi, g, i: (bi, g, i, 0, 0, 0)),
            pl.BlockSpec((None, s, HEAD_DIM), lambda bi, g, i: (bi, 0, g)),
            pl.BlockSpec((None, None, s // KV_CHUNK, VT_ROWS, KV_CHUNK),
                         lambda bi, g, i: (bi, g, 0, 0, 0)),
            pl.BlockSpec((META_PAD, HEAD_DIM), lambda bi, g, i: (0, g)),
            pl.BlockSpec((None, VT_ROWS, META_PAD), lambda bi, g, i: (g, 0, 0)),
        ],
        out_specs=pl.BlockSpec((None, Q_TILE, GQA_GROUP * HEAD_DIM), lambda bi, g, i: (bi, i, g)),
        out_shape=jax.ShapeDtypeStruct((b, s, ATTN_WIDTH), BF16),
        scratch_shapes=[
            pltpu.VMEM((2, GQA_GROUP, KV_CHUNK, Q_TILE), F32),
            pltpu.VMEM((2, GQA_GROUP, 1, Q_TILE), F32),
            pltpu.VMEM((GQA_GROUP, 1, Q_TILE), F32),
            pltpu.VMEM((GQA_GROUP, VT_ROWS, Q_TILE), F32),
        ],
        compiler_params=pltpu.CompilerParams(
            dimension_semantics=("arbitrary", "arbitrary", "arbitrary"),
            vmem_limit_bytes=_vmem_limit(vmem)),
        name="attention_bounded" if bounded else "attention",
    )(q, k, vt, k_meta, vt_meta)


def _pool_fold_kernel(wg_ref, ps_ref, wp_ref, o_ref):
    for gi in range(N_POOL_GROUPS):
        rows = slice(gi * POOL_GROUP_DIM, (gi + 1) * POOL_GROUP_DIM)
        scaled = wg_ref[gi] * ps_ref[:, rows]
        o_ref[rows, :] = jnp.dot(scaled, wp_ref[rows, :], preferred_element_type=F32,
                                 precision=lax.Precision.HIGHEST).astype(BF16)


def _pool_fold(w_pool_grp, pool_scale, w_pool_br):
    d = w_pool_br.shape[1]
    return pl.pallas_call(
        _pool_fold_kernel,
        grid=(1,),
        in_specs=[
            pl.BlockSpec((N_POOL_GROUPS, POOL_GROUP_DIM, POOL_GROUP_DIM), lambda i: (0, 0, 0)),
            pl.BlockSpec((1, POOL_WIDTH), lambda i: (0, 0)),
            pl.BlockSpec((POOL_WIDTH, d), lambda i: (0, 0)),
        ],
        out_specs=pl.BlockSpec((POOL_WIDTH, d), lambda i: (0, 0)),
        out_shape=jax.ShapeDtypeStruct((POOL_WIDTH, d), BF16),
        compiler_params=pltpu.CompilerParams(dimension_semantics=("arbitrary",)),
        name="pool_fold",
    )(w_pool_grp, pool_scale, w_pool_br)


def _post_mix_kernel(attn_ref, p_ref, prev_ref, next_ref, pm_ref, gate_ref, x_ref,
                     wa_ref, wp_ref, wo_ref, g_ref, o_ref, ext_sc):
    tile = x_ref.shape[0]
    i = pl.program_id(1)
    nt = pl.num_programs(1)
    sub = tile // MIX_SPLIT
    ext = sub + 2 * POOL_HALO

    ext_sc[0:POOL_HALO, :] = jnp.where(i == 0, pm_ref[...], prev_ref[...])
    ext_sc[POOL_HALO:POOL_HALO + tile, :] = p_ref[...]
    ext_sc[POOL_HALO + tile:, :] = jnp.where(i == nt - 1, 0.0, next_ref[...])

    def ahead(a, k):
        return pltpu.roll(a, ext - k, 0)

    tail_row = lax.broadcasted_iota(jnp.int32, (POOL_HALO, POOL_GROUP_DIM), 0)

    for part in range(MIX_SPLIT):
        rows = slice(part * sub, (part + 1) * sub)
        attn_br = jnp.dot(attn_ref[rows, :], wa_ref[...], preferred_element_type=F32)

        pooled = []
        for gi, w in enumerate(POOL_WINDOWS):
            cols = slice(gi * POOL_GROUP_DIM, (gi + 1) * POOL_GROUP_DIM)
            half = w // 2
            e = ext_sc[part * sub:part * sub + ext, cols]
            run, span = e, 1
            while span < half:
                run = run + ahead(run, span)
                span *= 2
            if half == POOL_HALO:
                total = run[0:sub] + run[POOL_HALO:POOL_HALO + sub]
            else:
                total = (run + pltpu.roll(run, half, 0))[POOL_HALO:POOL_HALO + sub]
            mean = total * (1.0 / w)
            if part == MIX_SPLIT - 1:
                over = jnp.maximum(tail_row + (half - POOL_HALO), 0)
                tail = total[sub - POOL_HALO:] / (w - over).astype(F32)
                tail = jnp.where(i == nt - 1, tail, mean[sub - POOL_HALO:])
                mean = jnp.concatenate([mean[:sub - POOL_HALO], tail], axis=0)
            pooled.append((mean - p_ref[rows, cols]).astype(BF16))
        pool_br = jnp.dot(jnp.concatenate(pooled, axis=-1), wp_ref[...],
                          preferred_element_type=F32)

        gates = gate_ref[rows, :]
        mixed = (gates[:, :D_MODEL].astype(F32) * attn_br
                 + gates[:, D_MODEL:].astype(F32) * pool_br).astype(BF16)
        y = jnp.dot(mixed, wo_ref[...], preferred_element_type=F32)
        o_ref[rows, :] = x_ref[rows, :] + y * _rms_scale(y) * g_ref[...]


def _post_mix(attn, p_in, p_meta, gates, x, w_attn_br, w_pool, w_out, post_g):
    b, s, d = x.shape
    t = MIX_TILE
    nt = s // t
    hb = t // POOL_HALO
    last_hb = s // POOL_HALO - 1
    in_specs = [
        pl.BlockSpec((None, t, ATTN_WIDTH), lambda bi, i: (bi, i, 0)),
        pl.BlockSpec((None, t, POOL_WIDTH), lambda bi, i: (bi, i, 0)),
        pl.BlockSpec((None, POOL_HALO, POOL_WIDTH),
                     lambda bi, i: (bi, jnp.maximum(i * hb - 1, 0), 0)),
        pl.BlockSpec((None, POOL_HALO, POOL_WIDTH),
                     lambda bi, i: (bi, jnp.minimum((i + 1) * hb, last_hb), 0)),
        pl.BlockSpec((POOL_HALO, POOL_WIDTH), lambda bi, i: (N_META // POOL_HALO - 1, 0)),
        pl.BlockSpec((None, t, GATE_WIDTH), lambda bi, i: (bi, i, 0)),
        pl.BlockSpec((None, t, d), lambda bi, i: (bi, i, 0)),
        _const_spec((ATTN_WIDTH, d)),
        _const_spec((POOL_WIDTH, d)),
        _const_spec((d, d)),
        _const_spec((1, d)),
    ]
    vmem = (2 * t * (ATTN_WIDTH * 2 + POOL_WIDTH * 4 + GATE_WIDTH * 2 + 2 * d * 4)
            + (ATTN_WIDTH + POOL_WIDTH + d) * d * 2 + (t + 2 * POOL_HALO) * POOL_WIDTH * 4
            + 6 * (t // MIX_SPLIT) * d * 4)
    return pl.pallas_call(
        _post_mix_kernel,
        grid=(b, nt),
        in_specs=in_specs,
        out_specs=pl.BlockSpec((None, t, d), lambda bi, i: (bi, i, 0)),
        out_shape=jax.ShapeDtypeStruct((b, s, d), F32),
        scratch_shapes=[pltpu.VMEM((t + 2 * POOL_HALO, POOL_WIDTH), F32)],
        compiler_params=pltpu.CompilerParams(
            dimension_semantics=("arbitrary", "arbitrary"),
            vmem_limit_bytes=_vmem_limit(vmem)),
        name="post_mix",
    )(attn, p_in, p_in, p_in, p_meta, gates, x, w_attn_br, w_pool, w_out, post_g)


def _mlp_kernel(x_ref, g1_ref, w1_ref, w2_ref, g2_ref, o_ref):
    sub = x_ref.shape[0] // MLP_SPLIT
    for part in range(MLP_SPLIT):
        rows = slice(part * sub, (part + 1) * sub)
        x = x_ref[rows, :]
        h = (x * _rms_scale(x) * g1_ref[...]).astype(BF16)
        z = None
        for f in range(D_FF // FF_CHUNK):
            cols = slice(f * FF_CHUNK, (f + 1) * FF_CHUNK)
            u = jnp.dot(h, w1_ref[:, cols], preferred_element_type=F32)
            u = jnp.square(jnp.maximum(u, 0.0)).astype(BF16)
            zf = jnp.dot(u, w2_ref[cols, :], preferred_element_type=F32)
            z = zf if z is None else z + zf
        o_ref[rows, :] = x + z * _rms_scale(z) * g2_ref[...]


def _mlp(x, pre_g, w1, w2, post_g):
    b, s, d = x.shape
    t = MLP_TILE
    sub = t // MLP_SPLIT
    vmem = 4 * t * d * 4 + 2 * d * D_FF * 2 + 4 * sub * FF_CHUNK * 4 + 2 * sub * d * 4
    return pl.pallas_call(
        _mlp_kernel,
        grid=(b, s // t),
        in_specs=[
            pl.BlockSpec((None, t, d), lambda bi, i: (bi, i, 0)),
            _const_spec((1, d)),
            _const_spec((d, D_FF)),
            _const_spec((D_FF, d)),
            _const_spec((1, d)),
        ],
        out_specs=pl.BlockSpec((None, t, d), lambda bi, i: (bi, i, 0)),
        out_shape=jax.ShapeDtypeStruct((b, s, d), F32),
        compiler_params=pltpu.CompilerParams(
            dimension_semantics=("arbitrary", "arbitrary"),
            vmem_limit_bytes=_vmem_limit(vmem)),
        name="mlp",
    )(x, pre_g, w1, w2, post_g)


def _rope_tables(seq):
    quarter = HEAD_DIM // 4
    rows = seq // GRID_W
    inv_freq = ROPE_THETA ** (-jnp.arange(quarter, dtype=F32) / quarter)
    ang_r = jnp.arange(rows, dtype=F32)[:, None] * inv_freq[None, :]
    ang_c = jnp.arange(GRID_W, dtype=F32)[:, None] * inv_freq[None, :]

    def expand(row_tab, col_tab, sign):
        r = jnp.broadcast_to(row_tab[:, None, None, None, :], (rows, GRID_W, 1, 2, quarter))
        c = jnp.broadcast_to(col_tab[None, :, None, None, :], (rows, GRID_W, 1, 2, quarter))
        tab = jnp.concatenate([r, c], axis=2) * sign.reshape(1, 1, 1, 2, 1)
        return tab.reshape(seq, HEAD_DIM)

    cos = expand(jnp.cos(ang_r), jnp.cos(ang_c), jnp.ones((2,), F32))
    sin = expand(jnp.sin(ang_r), jnp.sin(ang_c), jnp.array([-1.0, 1.0], F32))
    return cos, sin


def kernel(x, meta_tokens, pre_mix_g, q_norm_g, k_norm_g, w_in, w_attn_br, w_pool_grp,
           pool_scale, w_pool_br, w_out, post_mix_g, pre_mlp_g, w_mlp_in, w_mlp_out, post_mlp_g):
    b, s, d = x.shape
    assert d == D_MODEL and s % max(IN_TILE, MIX_TILE, MLP_TILE) == 0
    assert (Q_TILE % IN_TILE == 0 or IN_TILE % Q_TILE == 0) and Q_TILE % (IN_TILE // IN_SPLIT) == 0
    assert (IN_TILE // IN_SPLIT) % KV_CHUNK == 0 and s % GRID_W == 0
    assert s % Q_TILE == 0 and KV_UNROLL % 2 == 0 and MIX_TILE % (MIX_SPLIT * BF16_ROWS) == 0
    assert pre_mix_g.shape[0] == 1

    w_in_b = w_in[0].astype(BF16)
    pre_g = pre_mix_g[0][None, :]
    qg = q_norm_g[0][None, :]
    kg = k_norm_g[0][None, :]
    cos_tab, sin_tab = _rope_tables(s)

    q, k, vt, p_in, gates = _in_proj(x, pre_g, w_in_b, qg, kg, cos_tab, sin_tab)

    meta_pad = jnp.zeros((META_PAD, d), F32).at[:N_META].set(meta_tokens.astype(F32))
    k_meta, vt_meta, p_meta = _meta_proj(meta_pad, pre_g, w_in_b, kg)

    score_bound = HEAD_DIM * Q_SCALE * jnp.max(jnp.abs(qg)) * jnp.max(jnp.abs(kg))
    attn = lax.cond(
        score_bound <= SCORE_BOUND,
        functools.partial(_attention, bounded=True),
        functools.partial(_attention, bounded=False),
        q, k, vt, k_meta, vt_meta)

    w_pool = _pool_fold(w_pool_grp[0], pool_scale[0][None, :], w_pool_br[0])
    x1 = _post_mix(attn, p_in, p_meta, gates, x, w_attn_br[0].astype(BF16), w_pool,
                   w_out[0].astype(BF16), post_mix_g[0][None, :])

    return _mlp(x1, pre_mlp_g[0][None, :], w_mlp_in[0].astype(BF16),
                w_mlp_out[0].astype(BF16), post_mlp_g[0][None, :])
```

```python
import functools
import math

import jax
import jax.numpy as jnp
import numpy as np
from jax import lax
from jax.experimental import pallas as pl
from jax.experimental.pallas import tpu as pltpu

D_MODEL = 1024
N_META = 16
GRID_W = 64
HEAD_DIM = 128
N_Q_HEADS = 8
N_KV_HEADS = 2
GQA_GROUP = N_Q_HEADS // N_KV_HEADS
ATTN_WIDTH = N_Q_HEADS * HEAD_DIM
KV_WIDTH = N_KV_HEADS * HEAD_DIM
POOL_WINDOWS = (2, 4, 8, 16)
N_POOL_GROUPS = len(POOL_WINDOWS)
POOL_GROUP_DIM = 128
POOL_WIDTH = N_POOL_GROUPS * POOL_GROUP_DIM
GATE_WIDTH = 2 * D_MODEL
D_FF = 4 * D_MODEL
ROPE_THETA = 10000.0
NORM_EPS = 1e-6

Q_OFF = 0
K_OFF = Q_OFF + ATTN_WIDTH
V_OFF = K_OFF + KV_WIDTH
P_OFF = V_OFF + KV_WIDTH
G_OFF = P_OFF + POOL_WIDTH
IN_WIDTH = G_OFF + GATE_WIDTH

V7X_VMEM_BYTES = 64 * 1024 * 1024
SUBLANES = 8
LANES = 128
BF16_ROWS = 16
VT_ROWS = HEAD_DIM + BF16_ROWS
POOL_HALO = SUBLANES

IN_TILE = 1024
IN_SPLIT = 4
Q_TILE = 1024
KV_CHUNK = 256
KV_UNROLL = 10
META_PAD = KV_CHUNK
MIX_TILE = 1024
MIX_SPLIT = 4
MLP_TILE = 1024
MLP_SPLIT = 4
FF_CHUNK = 1024

Q_SCALE = math.log2(math.e) / math.sqrt(HEAD_DIM)
MASK_VALUE = -1e30
SCORE_BOUND = 60.0

F32 = jnp.float32
BF16 = jnp.bfloat16
NT_DIMS = (((1,), (1,)), ((), ()))


def _vmem_limit(nbytes):
    return int(min(V7X_VMEM_BYTES - (4 << 20), max(nbytes, 16 << 20)))


def _const_spec(shape):
    nd = len(shape)
    return pl.BlockSpec(shape, lambda *_: (0,) * nd, pipeline_mode=pl.Buffered(1))


def _rms_scale(t):
    return lax.rsqrt(jnp.mean(t * t, axis=-1, keepdims=True) + NORM_EPS)


def _norm_rope(t, gain, cos, sin, first_half):
    y = t * _rms_scale(t) * gain
    swapped = jnp.where(first_half, pltpu.roll(y, 96, 1), pltpu.roll(y, 32, 1))
    return y * cos + swapped * sin


def _in_proj_kernel(x_ref, g_ref, w_ref, qg_ref, kg_ref, cos_ref, sin_ref,
                    q_ref, k_ref, vt_ref, p_ref, gate_ref):
    tile = x_ref.shape[0]
    sub = tile // IN_SPLIT
    lane = lax.broadcasted_iota(jnp.int32, (sub, HEAD_DIM), 1)
    first_half = (lane & 32) == 0
    qg = qg_ref[...] * Q_SCALE
    kg = kg_ref[...]

    for part in range(IN_SPLIT):
        r0 = part * sub
        rows = slice(r0, r0 + sub)
        x = x_ref[rows, :]
        h = (x * _rms_scale(x) * g_ref[...]).astype(BF16)
        cos = cos_ref[rows, :]
        sin = sin_ref[rows, :]

        qp = jnp.dot(h, w_ref[:, Q_OFF:K_OFF], preferred_element_type=F32)
        qt, qr = divmod(r0, Q_TILE)
        for hd in range(N_Q_HEADS):
            o = _norm_rope(qp[:, hd * HEAD_DIM:(hd + 1) * HEAD_DIM], qg, cos, sin, first_half)
            grp, j = divmod(hd, GQA_GROUP)
            q_ref[grp, qt, j, qr:qr + sub, :] = o.astype(BF16)

        kp = jnp.dot(h, w_ref[:, K_OFF:V_OFF], preferred_element_type=F32)
        for hd in range(N_KV_HEADS):
            o = _norm_rope(kp[:, hd * HEAD_DIM:(hd + 1) * HEAD_DIM], kg, cos, sin, first_half)
            k_ref[rows, hd * HEAD_DIM:(hd + 1) * HEAD_DIM] = o.astype(BF16)

        vp = jnp.dot(h, w_ref[:, V_OFF:P_OFF], preferred_element_type=F32)
        for hd in range(N_KV_HEADS):
            vt = vp[:, hd * HEAD_DIM:(hd + 1) * HEAD_DIM].T.astype(BF16)
            for c in range(sub // KV_CHUNK):
                chunk = r0 // KV_CHUNK + c
                vt_ref[hd, chunk, 0:HEAD_DIM, :] = vt[:, c * KV_CHUNK:(c + 1) * KV_CHUNK]
                vt_ref[hd, chunk, HEAD_DIM:, :] = jnp.ones((BF16_ROWS, KV_CHUNK), BF16)

        p_ref[rows, :] = jnp.dot(h, w_ref[:, P_OFF:G_OFF], preferred_element_type=F32)

        gl = jnp.dot(h, w_ref[:, G_OFF:IN_WIDTH], preferred_element_type=F32)
        gate_ref[rows, :] = jax.nn.sigmoid(gl).astype(BF16)


def _in_proj(x, pre_g, w_in, qg, kg, cos_tab, sin_tab):
    b, s, d = x.shape
    t = IN_TILE
    nt = s // t
    q_tiles = max(1, t // Q_TILE)
    q_rows = min(t, Q_TILE)
    out_shape = (
        jax.ShapeDtypeStruct((b, N_KV_HEADS, s // Q_TILE, GQA_GROUP, Q_TILE, HEAD_DIM), BF16),
        jax.ShapeDtypeStruct((b, s, KV_WIDTH), BF16),
        jax.ShapeDtypeStruct((b, N_KV_HEADS, s // KV_CHUNK, VT_ROWS, KV_CHUNK), BF16),
        jax.ShapeDtypeStruct((b, s, POOL_WIDTH), F32),
        jax.ShapeDtypeStruct((b, s, GATE_WIDTH), BF16),
    )
    in_specs = [
        pl.BlockSpec((None, t, d), lambda bi, i: (bi, i, 0)),
        _const_spec((1, d)),
        _const_spec((d, IN_WIDTH)),
        _const_spec((1, HEAD_DIM)),
        _const_spec((1, HEAD_DIM)),
        pl.BlockSpec((t, HEAD_DIM), lambda bi, i: (i, 0)),
        pl.BlockSpec((t, HEAD_DIM), lambda bi, i: (i, 0)),
    ]
    out_specs = (
        pl.BlockSpec((None, N_KV_HEADS, q_tiles, GQA_GROUP, q_rows, HEAD_DIM),
                     lambda bi, i: (bi, 0, (i * t) // (q_tiles * Q_TILE), 0,
                                    ((i * t) % Q_TILE) // q_rows, 0)),
        pl.BlockSpec((None, t, KV_WIDTH), lambda bi, i: (bi, i, 0)),
        pl.BlockSpec((None, N_KV_HEADS, t // KV_CHUNK, VT_ROWS, KV_CHUNK),
                     lambda bi, i: (bi, 0, i, 0, 0)),
        pl.BlockSpec((None, t, POOL_WIDTH), lambda bi, i: (bi, i, 0)),
        pl.BlockSpec((None, t, GATE_WIDTH), lambda bi, i: (bi, i, 0)),
    )
    vmem = (2 * t * d * 4 + d * IN_WIDTH * 2
            + 2 * t * (ATTN_WIDTH * 2 + 2 * KV_WIDTH * 2 + POOL_WIDTH * 4 + GATE_WIDTH * 2)
            + 4 * (t // IN_SPLIT) * GATE_WIDTH * 4)
    return pl.pallas_call(
        _in_proj_kernel,
        grid=(b, nt),
        in_specs=in_specs,
        out_specs=out_specs,
        out_shape=out_shape,
        compiler_params=pltpu.CompilerParams(
            dimension_semantics=("arbitrary", "arbitrary"),
            vmem_limit_bytes=_vmem_limit(vmem)),
        name="in_proj",
    )(x, pre_g, w_in, qg, kg, cos_tab, sin_tab)


def _meta_proj_kernel(x_ref, g_ref, w_ref, kg_ref, k_ref, vt_ref, p_ref):
    x = x_ref[...]
    h = (x * _rms_scale(x) * g_ref[...]).astype(BF16)
    proj = jnp.dot(h, w_ref[...], preferred_element_type=F32)
    kg = kg_ref[...]
    for hd in range(N_KV_HEADS):
        t = proj[:, hd * HEAD_DIM:(hd + 1) * HEAD_DIM]
        k_ref[:, hd * HEAD_DIM:(hd + 1) * HEAD_DIM] = (t * _rms_scale(t) * kg).astype(BF16)
        v = proj[:, KV_WIDTH + hd * HEAD_DIM:KV_WIDTH + (hd + 1) * HEAD_DIM]
        vt_ref[hd, 0:HEAD_DIM, :] = v.T.astype(BF16)
        col = lax.broadcasted_iota(jnp.int32, (BF16_ROWS, META_PAD), 1)
        vt_ref[hd, HEAD_DIM:, :] = jnp.where(col < N_META, 1.0, 0.0).astype(BF16)
    p_ref[...] = proj[:, 2 * KV_WIDTH:]


def _meta_proj(meta_pad, pre_g, w_in, kg):
    width = G_OFF - K_OFF
    assert width == D_MODEL and K_OFF == width
    return pl.pallas_call(
        _meta_proj_kernel,
        grid=(1,),
        in_specs=[
            pl.BlockSpec((META_PAD, D_MODEL), lambda i: (0, 0)),
            pl.BlockSpec((1, D_MODEL), lambda i: (0, 0)),
            pl.BlockSpec((D_MODEL, width), lambda i: (0, 1)),
            pl.BlockSpec((1, HEAD_DIM), lambda i: (0, 0)),
        ],
        out_specs=(
            pl.BlockSpec((META_PAD, KV_WIDTH), lambda i: (0, 0)),
            pl.BlockSpec((N_KV_HEADS, VT_ROWS, META_PAD), lambda i: (0, 0, 0)),
            pl.BlockSpec((META_PAD, POOL_WIDTH), lambda i: (0, 0)),
        ),
        out_shape=(
            jax.ShapeDtypeStruct((META_PAD, KV_WIDTH), BF16),
            jax.ShapeDtypeStruct((N_KV_HEADS, VT_ROWS, META_PAD), BF16),
            jax.ShapeDtypeStruct((META_PAD, POOL_WIDTH), F32),
        ),
        compiler_params=pltpu.CompilerParams(dimension_semantics=("arbitrary",)),
        name="meta_proj",
    )(meta_pad, pre_g, w_in, kg)


def _attn_kernel(q_ref, k_ref, vt_ref, km_ref, vtm_ref, o_ref, s_sc, cmax_sc, m_sc, acc_sc, *,
                 bounded):
    n_real = k_ref.shape[0] // KV_CHUNK

    def q_head(j):
        return q_ref[j]

    def produce(j, c, slot):
        if isinstance(c, int) and c == n_real:
            kc = km_ref[...]
        elif isinstance(c, int):
            kc = k_ref[c * KV_CHUNK:(c + 1) * KV_CHUNK, :]
        else:
            kc = k_ref[pl.ds(pl.multiple_of(c * KV_CHUNK, KV_CHUNK), KV_CHUNK), :]
        s = lax.dot_general(kc, q_head(j), NT_DIMS, preferred_element_type=F32)
        if isinstance(c, int) and c == n_real:
            row = lax.broadcasted_iota(jnp.int32, s.shape, 0)
            s = jnp.where(row < N_META, s, MASK_VALUE)
        s_sc[slot, j] = s
        if not bounded:
            cmax_sc[slot, j] = jnp.max(s, axis=0, keepdims=True)

    def consume(j, c, slot):
        vt = vtm_ref[...] if isinstance(c, int) and c == n_real else vt_ref[c]
        if bounded:
            p = jnp.exp2(s_sc[slot, j]).astype(BF16)
            acc_sc[j] += jnp.dot(vt, p, preferred_element_type=F32)
            return
        m_prev = m_sc[j]
        m_new = jnp.maximum(m_prev, cmax_sc[slot, j])
        alpha = jnp.exp2(m_prev - m_new)
        p = jnp.exp2(s_sc[slot, j] - m_new).astype(BF16)
        pv = jnp.dot(vt, p, preferred_element_type=F32)
        acc_sc[j] = alpha * acc_sc[j] + pv
        m_sc[j] = m_new

    for j in range(GQA_GROUP):
        if not bounded:
            m_sc[j] = jnp.full((1, Q_TILE), MASK_VALUE, F32)
        acc_sc[j] = jnp.zeros((VT_ROWS, Q_TILE), F32)
        produce(j, 0, 0)

    def steps(c0, count):
        for u in range(count):
            for j in range(GQA_GROUP):
                produce(j, c0 + u + 1, (u + 1) % 2)
                consume(j, c0 + u, u % 2)

    def body(i, carry):
        steps(i * KV_UNROLL, KV_UNROLL)
        return carry

    n_loop = n_real // KV_UNROLL
    lax.fori_loop(0, n_loop, body, 0)
    done = n_loop * KV_UNROLL
    steps(done, n_real - done)
    for j in range(GQA_GROUP):
        consume(j, n_real, n_real % 2)
        acc = acc_sc[j]
        out = acc[:HEAD_DIM] / acc[HEAD_DIM:HEAD_DIM + 1]
        o_ref[:, j * HEAD_DIM:(j + 1) * HEAD_DIM] = out.T.astype(BF16)


def _attention(q, k, vt, k_meta, vt_meta, *, bounded):
    b, _, nq, _, _, _ = q.shape
    s = k.shape[1]
    vmem = (2 * (s * HEAD_DIM * 2 + s * VT_ROWS * 2 + 2 * Q_TILE * GQA_GROUP * HEAD_DIM * 2)
            + GQA_GROUP * Q_TILE * 4 * (2 * KV_CHUNK + VT_ROWS + 3 * SUBLANES)
            + 8 * KV_CHUNK * Q_TILE * 4)
    return pl.pallas_call(
        functools.partial(_attn_kernel, bounded=bounded),
        grid=(b, N_KV_HEADS, nq),
        in_specs=[
            pl.BlockSpec((None, None, None, GQA_GROUP, Q_TILE, HEAD_DIM),
                         lambda bi, g, i: (bi, g, i, 0, 0, 0)),
            pl.BlockSpec((None, s, HEAD_DIM), lambda bi, g, i: (bi, 0, g)),
            pl.BlockSpec((None, None, s // KV_CHUNK, VT_ROWS, KV_CHUNK),
                         lambda bi, g, i: (bi, g, 0, 0, 0)),
            pl.BlockSpec((META_PAD, HEAD_DIM), lambda bi, g, i: (0, g)),
            pl.BlockSpec((None, VT_ROWS, META_PAD), lambda bi, g, i: (g, 0, 0)),
        ],
        out_specs=pl.BlockSpec((None, Q_TILE, GQA_GROUP * HEAD_DIM), lambda bi, g, i: (bi, i, g)),
        out_shape=jax.ShapeDtypeStruct((b, s, ATTN_WIDTH), BF16),
        scratch_shapes=[
            pltpu.VMEM((2, GQA_GROUP, KV_CHUNK, Q_TILE), F32),
            pltpu.VMEM((2, GQA_GROUP, 1, Q_TILE), F32),
            pltpu.VMEM((GQA_GROUP, 1, Q_TILE), F32),
            pltpu.VMEM((GQA_GROUP, VT_ROWS, Q_TILE), F32),
        ],
        compiler_params=pltpu.CompilerParams(
            dimension_semantics=("arbitrary", "arbitrary", "arbitrary"),
            vmem_limit_bytes=_vmem_limit(vmem)),
        name="attention_bounded" if bounded else "attention",
    )(q, k, vt, k_meta, vt_meta)


def _pool_fold_kernel(wg_ref, ps_ref, wp_ref, o_ref):
    for gi in range(N_POOL_GROUPS):
        rows = slice(gi * POOL_GROUP_DIM, (gi + 1) * POOL_GROUP_DIM)
        scaled = wg_ref[gi] * ps_ref[:, rows]
        o_ref[rows, :] = jnp.dot(scaled, wp_ref[rows, :], preferred_element_type=F32,
                                 precision=lax.Precision.HIGHEST).astype(BF16)


def _pool_fold(w_pool_grp, pool_scale, w_pool_br):
    d = w_pool_br.shape[1]
    return pl.pallas_call(
        _pool_fold_kernel,
        grid=(1,),
        in_specs=[
            pl.BlockSpec((N_POOL_GROUPS, POOL_GROUP_DIM, POOL_GROUP_DIM), lambda i: (0, 0, 0)),
            pl.BlockSpec((1, POOL_WIDTH), lambda i: (0, 0)),
            pl.BlockSpec((POOL_WIDTH, d), lambda i: (0, 0)),
        ],
        out_specs=pl.BlockSpec((POOL_WIDTH, d), lambda i: (0, 0)),
        out_shape=jax.ShapeDtypeStruct((POOL_WIDTH, d), BF16),
        compiler_params=pltpu.CompilerParams(dimension_semantics=("arbitrary",)),
        name="pool_fold",
    )(w_pool_grp, pool_scale, w_pool_br)


def _mlp_rows(x, g1_ref, w1_ref, w2_ref, g2_ref):
    h = (x * _rms_scale(x) * g1_ref[...]).astype(BF16)
    z = None
    for f in range(D_FF // FF_CHUNK):
        cols = slice(f * FF_CHUNK, (f + 1) * FF_CHUNK)
        u = jnp.dot(h, w1_ref[:, cols], preferred_element_type=F32)
        u = jnp.square(jnp.maximum(u, 0.0)).astype(BF16)
        zf = jnp.dot(u, w2_ref[cols, :], preferred_element_type=F32)
        z = zf if z is None else z + zf
    return x + z * _rms_scale(z) * g2_ref[...]


def _post_mix_kernel(attn_ref, p_ref, prev_ref, next_ref, pm_ref, gate_ref, x_ref,
                     wa_ref, wp_ref, wo_ref, g_ref, o_ref, ext_sc):
    tile = x_ref.shape[0]
    i = pl.program_id(1)
    nt = pl.num_programs(1)
    sub = tile // MIX_SPLIT
    ext = sub + 2 * POOL_HALO

    ext_sc[0:POOL_HALO, :] = jnp.where(i == 0, pm_ref[...], prev_ref[...])
    ext_sc[POOL_HALO:POOL_HALO + tile, :] = p_ref[...]
    ext_sc[POOL_HALO + tile:, :] = jnp.where(i == nt - 1, 0.0, next_ref[...])

    def ahead(a, k):
        return pltpu.roll(a, ext - k, 0)

    tail_row = lax.broadcasted_iota(jnp.int32, (POOL_HALO, POOL_GROUP_DIM), 0)

    for part in range(MIX_SPLIT):
        rows = slice(part * sub, (part + 1) * sub)
        attn_br = jnp.dot(attn_ref[rows, :], wa_ref[...], preferred_element_type=F32)

        pooled = []
        for gi, w in enumerate(POOL_WINDOWS):
            cols = slice(gi * POOL_GROUP_DIM, (gi + 1) * POOL_GROUP_DIM)
            half = w // 2
            e = ext_sc[part * sub:part * sub + ext, cols]
            run, span = e, 1
            while span < half:
                run = run + ahead(run, span)
                span *= 2
            if half == POOL_HALO:
                total = run[0:sub] + run[POOL_HALO:POOL_HALO + sub]
            else:
                total = (run + pltpu.roll(run, half, 0))[POOL_HALO:POOL_HALO + sub]
            mean = total * (1.0 / w)
            if part == MIX_SPLIT - 1:
                over = jnp.maximum(tail_row + (half - POOL_HALO), 0)
                tail = total[sub - POOL_HALO:] / (w - over).astype(F32)
                tail = jnp.where(i == nt - 1, tail, mean[sub - POOL_HALO:])
                mean = jnp.concatenate([mean[:sub - POOL_HALO], tail], axis=0)
            pooled.append((mean - p_ref[rows, cols]).astype(BF16))
        pool_br = jnp.dot(jnp.concatenate(pooled, axis=-1), wp_ref[...],
                          preferred_element_type=F32)

        gates = gate_ref[rows, :]
        mixed = (gates[:, :D_MODEL].astype(F32) * attn_br
                 + gates[:, D_MODEL:].astype(F32) * pool_br).astype(BF16)
        y = jnp.dot(mixed, wo_ref[...], preferred_element_type=F32)
        o_ref[rows, :] = x_ref[rows, :] + y * _rms_scale(y) * g_ref[...]


def _post_mix(attn, p_in, p_meta, gates, x, w_attn_br, w_pool, w_out, post_g):
    b, s, d = x.shape
    t = MIX_TILE
    nt = s // t
    hb = t // POOL_HALO
    last_hb = s // POOL_HALO - 1
    in_specs = [
        pl.BlockSpec((None, t, ATTN_WIDTH), lambda bi, i: (bi, i, 0)),
        pl.BlockSpec((None, t, POOL_WIDTH), lambda bi, i: (bi, i, 0)),
        pl.BlockSpec((None, POOL_HALO, POOL_WIDTH),
                     lambda bi, i: (bi, jnp.maximum(i * hb - 1, 0), 0)),
        pl.BlockSpec((None, POOL_HALO, POOL_WIDTH),
                     lambda bi, i: (bi, jnp.minimum((i + 1) * hb, last_hb), 0)),
        pl.BlockSpec((POOL_HALO, POOL_WIDTH), lambda bi, i: (N_META // POOL_HALO - 1, 0)),
        pl.BlockSpec((None, t, GATE_WIDTH), lambda bi, i: (bi, i, 0)),
        pl.BlockSpec((None, t, d), lambda bi, i: (bi, i, 0)),
        _const_spec((ATTN_WIDTH, d)),
        _const_spec((POOL_WIDTH, d)),
        _const_spec((d, d)),
        _const_spec((1, d)),
    ]
    vmem = (2 * t * (ATTN_WIDTH * 2 + POOL_WIDTH * 4 + GATE_WIDTH * 2 + 2 * d * 4)
            + (ATTN_WIDTH + POOL_WIDTH + d) * d * 2 + (t + 2 * POOL_HALO) * POOL_WIDTH * 4
            + 6 * (t // MIX_SPLIT) * d * 4)
    return pl.pallas_call(
        _post_mix_kernel,
        grid=(b, nt),
        in_specs=in_specs,
        out_specs=pl.BlockSpec((None, t, d), lambda bi, i: (bi, i, 0)),
        out_shape=jax.ShapeDtypeStruct((b, s, d), F32),
        scratch_shapes=[pltpu.VMEM((t + 2 * POOL_HALO, POOL_WIDTH), F32)],
        compiler_params=pltpu.CompilerParams(
            dimension_semantics=("arbitrary", "arbitrary"),
            vmem_limit_bytes=_vmem_limit(vmem)),
        name="post_mix",
    )(attn, p_in, p_in, p_in, p_meta, gates, x, w_attn_br, w_pool, w_out, post_g)


def _mlp_kernel(x_ref, g1_ref, w1_ref, w2_ref, g2_ref, o_ref):
    sub = x_ref.shape[0] // MLP_SPLIT
    for part in range(MLP_SPLIT):
        rows = slice(part * sub, (part + 1) * sub)
        o_ref[rows, :] = _mlp_rows(x_ref[rows, :], g1_ref, w1_ref, w2_ref, g2_ref)


def _mlp(x, pre_g, w1, w2, post_g):
    b, s, d = x.shape
    t = MLP_TILE
    sub = t // MLP_SPLIT
    vmem = 4 * t * d * 4 + 2 * d * D_FF * 2 + 4 * sub * FF_CHUNK * 4 + 2 * sub * d * 4
    return pl.pallas_call(
        _mlp_kernel,
        grid=(b, s // t),
        in_specs=[
            pl.BlockSpec((None, t, d), lambda bi, i: (bi, i, 0)),
            _const_spec((1, d)),
            _const_spec((d, D_FF)),
            _const_spec((D_FF, d)),
            _const_spec((1, d)),
        ],
        out_specs=pl.BlockSpec((None, t, d), lambda bi, i: (bi, i, 0)),
        out_shape=jax.ShapeDtypeStruct((b, s, d), F32),
        compiler_params=pltpu.CompilerParams(
            dimension_semantics=("arbitrary", "arbitrary"),
            vmem_limit_bytes=_vmem_limit(vmem)),
        name="mlp",
    )(x, pre_g, w1, w2, post_g)


def _rope_tables(seq):
    quarter = HEAD_DIM // 4
    inv_freq = ROPE_THETA ** (-np.arange(quarter, dtype=np.float64) / quarter)
    pos = np.arange(seq)
    ang_r = (pos // GRID_W)[:, None] * inv_freq[None, :]
    ang_c = (pos % GRID_W)[:, None] * inv_freq[None, :]
    cos = np.concatenate([np.cos(ang_r), np.cos(ang_r), np.cos(ang_c), np.cos(ang_c)], -1)
    sin = np.concatenate([-np.sin(ang_r), np.sin(ang_r), -np.sin(ang_c), np.sin(ang_c)], -1)
    return jnp.asarray(cos, F32), jnp.asarray(sin, F32)


def kernel(x, meta_tokens, pre_mix_g, q_norm_g, k_norm_g, w_in, w_attn_br, w_pool_grp,
           pool_scale, w_pool_br, w_out, post_mix_g, pre_mlp_g, w_mlp_in, w_mlp_out, post_mlp_g):
    b, s, d = x.shape
    assert d == D_MODEL and s % max(IN_TILE, MIX_TILE, MLP_TILE) == 0
    assert (Q_TILE % IN_TILE == 0 or IN_TILE % Q_TILE == 0) and Q_TILE % (IN_TILE // IN_SPLIT) == 0
    assert (IN_TILE // IN_SPLIT) % KV_CHUNK == 0 and s % GRID_W == 0
    assert s % Q_TILE == 0 and KV_UNROLL % 2 == 0 and MIX_TILE % (MIX_SPLIT * BF16_ROWS) == 0
    assert pre_mix_g.shape[0] == 1

    w_in_b = w_in[0].astype(BF16)
    pre_g = pre_mix_g[0][None, :]
    qg = q_norm_g[0][None, :]
    kg = k_norm_g[0][None, :]
    cos_tab, sin_tab = _rope_tables(s)

    q, k, vt, p_in, gates = _in_proj(x, pre_g, w_in_b, qg, kg, cos_tab, sin_tab)

    meta_pad = jnp.zeros((META_PAD, d), F32).at[:N_META].set(meta_tokens.astype(F32))
    k_meta, vt_meta, p_meta = _meta_proj(meta_pad, pre_g, w_in_b, kg)

    score_bound = HEAD_DIM * Q_SCALE * jnp.max(jnp.abs(qg)) * jnp.max(jnp.abs(kg))
    attn = lax.cond(
        score_bound <= SCORE_BOUND,
        functools.partial(_attention, bounded=True),
        functools.partial(_attention, bounded=False),
        q, k, vt, k_meta, vt_meta)

    w_pool = _pool_fold(w_pool_grp[0], pool_scale[0][None, :], w_pool_br[0])
    x1 = _post_mix(attn, p_in, p_meta, gates, x, w_attn_br[0].astype(BF16), w_pool,
                   w_out[0].astype(BF16), post_mix_g[0][None, :])

    return _mlp(x1, pre_mlp_g[0][None, :], w_mlp_in[0].astype(BF16),
                w_mlp_out[0].astype(BF16), post_mlp_g[0][None, :])
```

```python
import functools
import math

import jax
import jax.numpy as jnp
import numpy as np
from jax import lax
from jax.experimental import pallas as pl
from jax.experimental.pallas import tpu as pltpu

D_MODEL = 1024
N_META = 16
GRID_W = 64
HEAD_DIM = 128
N_Q_HEADS = 8
N_KV_HEADS = 2
GQA_GROUP = N_Q_HEADS // N_KV_HEADS
ATTN_WIDTH = N_Q_HEADS * HEAD_DIM
KV_WIDTH = N_KV_HEADS * HEAD_DIM
POOL_WINDOWS = (2, 4, 8, 16)
N_POOL_GROUPS = len(POOL_WINDOWS)
POOL_GROUP_DIM = 128
POOL_WIDTH = N_POOL_GROUPS * POOL_GROUP_DIM
GATE_WIDTH = 2 * D_MODEL
D_FF = 4 * D_MODEL
ROPE_THETA = 10000.0
NORM_EPS = 1e-6

Q_OFF = 0
K_OFF = Q_OFF + ATTN_WIDTH
V_OFF = K_OFF + KV_WIDTH
P_OFF = V_OFF + KV_WIDTH
G_OFF = P_OFF + POOL_WIDTH
IN_WIDTH = G_OFF + GATE_WIDTH

V7X_VMEM_BYTES = 64 * 1024 * 1024
SUBLANES = 8
LANES = 128
BF16_ROWS = 16
VT_ROWS = HEAD_DIM + BF16_ROWS
POOL_HALO = SUBLANES

IN_TILE = 1024
IN_SPLIT = 4
Q_TILE = 1024
KV_CHUNK = 256
KV_UNROLL = 10
META_PAD = KV_CHUNK
MIX_TILE = 1024
MIX_SPLIT = 4
MLP_TILE = 1024
MLP_SPLIT = 4
FF_CHUNK = 1024

Q_SCALE = math.log2(math.e) / math.sqrt(HEAD_DIM)
MASK_VALUE = -1e30
SCORE_BOUND = 60.0

F32 = jnp.float32
BF16 = jnp.bfloat16
NT_DIMS = (((1,), (1,)), ((), ()))


def _vmem_limit(nbytes):
    return int(min(V7X_VMEM_BYTES - (4 << 20), max(nbytes, 16 << 20)))


def _const_spec(shape):
    nd = len(shape)
    return pl.BlockSpec(shape, lambda *_: (0,) * nd, pipeline_mode=pl.Buffered(1))


def _rms_scale(t):
    return lax.rsqrt(jnp.mean(t * t, axis=-1, keepdims=True) + NORM_EPS)


def _norm_rope(t, gain, cos, sin, first_half):
    y = t * _rms_scale(t) * gain
    swapped = jnp.where(first_half, pltpu.roll(y, 96, 1), pltpu.roll(y, 32, 1))
    return y * cos + swapped * sin


def _in_proj_kernel(x_ref, g_ref, w_ref, qg_ref, kg_ref, cos_ref, sin_ref, *refs, n_cast):
    cast_in = refs[:n_cast]
    q_ref, k_ref, vt_ref, p_ref, gate_ref = refs[n_cast:n_cast + 5]
    for src, dst in zip(cast_in, refs[n_cast + 5:]):
        dst[...] = src[...].astype(BF16)

    tile = x_ref.shape[0]
    sub = tile // IN_SPLIT
    lane = lax.broadcasted_iota(jnp.int32, (sub, HEAD_DIM), 1)
    first_half = (lane & 32) == 0
    qg = qg_ref[...] * Q_SCALE
    kg = kg_ref[...]

    for part in range(IN_SPLIT):
        r0 = part * sub
        rows = slice(r0, r0 + sub)
        x = x_ref[rows, :]
        h = (x * _rms_scale(x) * g_ref[...]).astype(BF16)
        cos = cos_ref[rows, :]
        sin = sin_ref[rows, :]

        qp = jnp.dot(h, w_ref[:, Q_OFF:K_OFF], preferred_element_type=F32)
        qt, qr = divmod(r0, Q_TILE)
        for hd in range(N_Q_HEADS):
            o = _norm_rope(qp[:, hd * HEAD_DIM:(hd + 1) * HEAD_DIM], qg, cos, sin, first_half)
            grp, j = divmod(hd, GQA_GROUP)
            q_ref[grp, qt, j, qr:qr + sub, :] = o.astype(BF16)

        kp = jnp.dot(h, w_ref[:, K_OFF:V_OFF], preferred_element_type=F32)
        for hd in range(N_KV_HEADS):
            o = _norm_rope(kp[:, hd * HEAD_DIM:(hd + 1) * HEAD_DIM], kg, cos, sin, first_half)
            k_ref[rows, hd * HEAD_DIM:(hd + 1) * HEAD_DIM] = o.astype(BF16)

        vp = jnp.dot(h, w_ref[:, V_OFF:P_OFF], preferred_element_type=F32)
        for hd in range(N_KV_HEADS):
            vt = vp[:, hd * HEAD_DIM:(hd + 1) * HEAD_DIM].T.astype(BF16)
            for c in range(sub // KV_CHUNK):
                chunk = r0 // KV_CHUNK + c
                vt_ref[hd, chunk, 0:HEAD_DIM, :] = vt[:, c * KV_CHUNK:(c + 1) * KV_CHUNK]
                vt_ref[hd, chunk, HEAD_DIM:, :] = jnp.ones((BF16_ROWS, KV_CHUNK), BF16)

        p_ref[rows, :] = jnp.dot(h, w_ref[:, P_OFF:G_OFF], preferred_element_type=F32)

        gl = jnp.dot(h, w_ref[:, G_OFF:IN_WIDTH], preferred_element_type=F32)
        gate_ref[rows, :] = jax.nn.sigmoid(gl).astype(BF16)


def _in_proj(x, pre_g, w_in, qg, kg, cos_tab, sin_tab, cast_weights):
    b, s, d = x.shape
    t = IN_TILE
    nt = s // t
    steps = b * nt
    cast_specs = []
    for w in cast_weights:
        rows = w.shape[0] // steps
        assert w.shape[0] == rows * steps and rows % BF16_ROWS == 0
        cast_specs.append(pl.BlockSpec((rows, w.shape[1]), lambda bi, i: (bi * nt + i, 0)))
    q_tiles = max(1, t // Q_TILE)
    q_rows = min(t, Q_TILE)
    out_shape = (
        jax.ShapeDtypeStruct((b, N_KV_HEADS, s // Q_TILE, GQA_GROUP, Q_TILE, HEAD_DIM), BF16),
        jax.ShapeDtypeStruct((b, s, KV_WIDTH), BF16),
        jax.ShapeDtypeStruct((b, N_KV_HEADS, s // KV_CHUNK, VT_ROWS, KV_CHUNK), BF16),
        jax.ShapeDtypeStruct((b, s, POOL_WIDTH), F32),
        jax.ShapeDtypeStruct((b, s, GATE_WIDTH), BF16),
    )
    in_specs = [
        pl.BlockSpec((None, t, d), lambda bi, i: (bi, i, 0)),
        _const_spec((1, d)),
        _const_spec((d, IN_WIDTH)),
        _const_spec((1, HEAD_DIM)),
        _const_spec((1, HEAD_DIM)),
        pl.BlockSpec((t, HEAD_DIM), lambda bi, i: (i, 0)),
        pl.BlockSpec((t, HEAD_DIM), lambda bi, i: (i, 0)),
    ] + cast_specs
    out_shape = out_shape + tuple(jax.ShapeDtypeStruct(w.shape, BF16) for w in cast_weights)
    out_specs = (
        pl.BlockSpec((None, N_KV_HEADS, q_tiles, GQA_GROUP, q_rows, HEAD_DIM),
                     lambda bi, i: (bi, 0, (i * t) // (q_tiles * Q_TILE), 0,
                                    ((i * t) % Q_TILE) // q_rows, 0)),
        pl.BlockSpec((None, t, KV_WIDTH), lambda bi, i: (bi, i, 0)),
        pl.BlockSpec((None, N_KV_HEADS, t // KV_CHUNK, VT_ROWS, KV_CHUNK),
                     lambda bi, i: (bi, 0, i, 0, 0)),
        pl.BlockSpec((None, t, POOL_WIDTH), lambda bi, i: (bi, i, 0)),
        pl.BlockSpec((None, t, GATE_WIDTH), lambda bi, i: (bi, i, 0)),
    ) + tuple(cast_specs)
    vmem = (2 * t * d * 4 + d * IN_WIDTH * 2
            + 2 * t * (ATTN_WIDTH * 2 + 2 * KV_WIDTH * 2 + POOL_WIDTH * 4 + GATE_WIDTH * 2)
            + 4 * (t // IN_SPLIT) * GATE_WIDTH * 4
            + 2 * sum(w.size * 6 for w in cast_weights) // steps)
    outs = pl.pallas_call(
        functools.partial(_in_proj_kernel, n_cast=len(cast_weights)),
        grid=(b, nt),
        in_specs=in_specs,
        out_specs=out_specs,
        out_shape=out_shape,
        compiler_params=pltpu.CompilerParams(
            dimension_semantics=("arbitrary", "arbitrary"),
            vmem_limit_bytes=_vmem_limit(vmem)),
        name="in_proj",
    )(x, pre_g, w_in, qg, kg, cos_tab, sin_tab, *cast_weights)
    return outs[:5], outs[5:]


def _meta_proj_kernel(x_ref, g_ref, w_ref, kg_ref, k_ref, vt_ref, p_ref):
    x = x_ref[...]
    h = (x * _rms_scale(x) * g_ref[...]).astype(BF16)
    proj = jnp.dot(h, w_ref[...], preferred_element_type=F32)
    kg = kg_ref[...]
    for hd in range(N_KV_HEADS):
        t = proj[:, hd * HEAD_DIM:(hd + 1) * HEAD_DIM]
        k_ref[:, hd * HEAD_DIM:(hd + 1) * HEAD_DIM] = (t * _rms_scale(t) * kg).astype(BF16)
        v = proj[:, KV_WIDTH + hd * HEAD_DIM:KV_WIDTH + (hd + 1) * HEAD_DIM]
        vt_ref[hd, 0:HEAD_DIM, :] = v.T.astype(BF16)
        col = lax.broadcasted_iota(jnp.int32, (BF16_ROWS, META_PAD), 1)
        vt_ref[hd, HEAD_DIM:, :] = jnp.where(col < N_META, 1.0, 0.0).astype(BF16)
    p_ref[...] = proj[:, 2 * KV_WIDTH:]


def _meta_proj(meta_pad, pre_g, w_in, kg):
    width = G_OFF - K_OFF
    assert width == D_MODEL and K_OFF == width
    return pl.pallas_call(
        _meta_proj_kernel,
        grid=(1,),
        in_specs=[
            pl.BlockSpec((META_PAD, D_MODEL), lambda i: (0, 0)),
            pl.BlockSpec((1, D_MODEL), lambda i: (0, 0)),
            pl.BlockSpec((D_MODEL, width), lambda i: (0, 1)),
            pl.BlockSpec((1, HEAD_DIM), lambda i: (0, 0)),
        ],
        out_specs=(
            pl.BlockSpec((META_PAD, KV_WIDTH), lambda i: (0, 0)),
            pl.BlockSpec((N_KV_HEADS, VT_ROWS, META_PAD), lambda i: (0, 0, 0)),
            pl.BlockSpec((META_PAD, POOL_WIDTH), lambda i: (0, 0)),
        ),
        out_shape=(
            jax.ShapeDtypeStruct((META_PAD, KV_WIDTH), BF16),
            jax.ShapeDtypeStruct((N_KV_HEADS, VT_ROWS, META_PAD), BF16),
            jax.ShapeDtypeStruct((META_PAD, POOL_WIDTH), F32),
        ),
        compiler_params=pltpu.CompilerParams(dimension_semantics=("arbitrary",)),
        name="meta_proj",
    )(meta_pad, pre_g, w_in, kg)


def _attn_kernel(q_ref, k_ref, vt_ref, km_ref, vtm_ref, o_ref, s_sc, cmax_sc, m_sc, acc_sc, *,
                 bounded):
    n_real = k_ref.shape[0] // KV_CHUNK

    def q_head(j):
        return q_ref[j]

    def produce(j, c, slot):
        if isinstance(c, int) and c == n_real:
            kc = km_ref[...]
        elif isinstance(c, int):
            kc = k_ref[c * KV_CHUNK:(c + 1) * KV_CHUNK, :]
        else:
            kc = k_ref[pl.ds(pl.multiple_of(c * KV_CHUNK, KV_CHUNK), KV_CHUNK), :]
        s = lax.dot_general(kc, q_head(j), NT_DIMS, preferred_element_type=F32)
        if isinstance(c, int) and c == n_real:
            row = lax.broadcasted_iota(jnp.int32, s.shape, 0)
            s = jnp.where(row < N_META, s, MASK_VALUE)
        s_sc[slot, j] = s
        if not bounded:
            cmax_sc[slot, j] = jnp.max(s, axis=0, keepdims=True)

    def consume(j, c, slot):
        vt = vtm_ref[...] if isinstance(c, int) and c == n_real else vt_ref[c]
        if bounded:
            p = jnp.exp2(s_sc[slot, j]).astype(BF16)
            acc_sc[j] += jnp.dot(vt, p, preferred_element_type=F32)
            return
        m_prev = m_sc[j]
        m_new = jnp.maximum(m_prev, cmax_sc[slot, j])
        alpha = jnp.exp2(m_prev - m_new)
        p = jnp.exp2(s_sc[slot, j] - m_new).astype(BF16)
        pv = jnp.dot(vt, p, preferred_element_type=F32)
        acc_sc[j] = alpha * acc_sc[j] + pv
        m_sc[j] = m_new

    for j in range(GQA_GROUP):
        if not bounded:
            m_sc[j] = jnp.full((1, Q_TILE), MASK_VALUE, F32)
        acc_sc[j] = jnp.zeros((VT_ROWS, Q_TILE), F32)
        produce(j, 0, 0)

    def steps(c0, count):
        for u in range(count):
            for j in range(GQA_GROUP):
                produce(j, c0 + u + 1, (u + 1) % 2)
                consume(j, c0 + u, u % 2)

    def body(i, carry):
        steps(i * KV_UNROLL, KV_UNROLL)
        return carry

    n_loop = n_real // KV_UNROLL
    lax.fori_loop(0, n_loop, body, 0)
    done = n_loop * KV_UNROLL
    steps(done, n_real - done)
    for j in range(GQA_GROUP):
        consume(j, n_real, n_real % 2)
        acc = acc_sc[j]
        out = acc[:HEAD_DIM] / acc[HEAD_DIM:HEAD_DIM + 1]
        o_ref[:, j * HEAD_DIM:(j + 1) * HEAD_DIM] = out.T.astype(BF16)


def _attention(q, k, vt, k_meta, vt_meta, *, bounded):
    b, _, nq, _, _, _ = q.shape
    s = k.shape[1]
    vmem = (2 * (s * HEAD_DIM * 2 + s * VT_ROWS * 2 + 2 * Q_TILE * GQA_GROUP * HEAD_DIM * 2)
            + GQA_GROUP * Q_TILE * 4 * (2 * KV_CHUNK + VT_ROWS + 3 * SUBLANES)
            + 8 * KV_CHUNK * Q_TILE * 4)
    return pl.pallas_call(
        functools.partial(_attn_kernel, bounded=bounded),
        grid=(b, N_KV_HEADS, nq),
        in_specs=[
            pl.BlockSpec((None, None, None, GQA_GROUP, Q_TILE, HEAD_DIM),
                         lambda bi, g, i: (bi, g, i, 0, 0, 0)),
            pl.BlockSpec((None, s, HEAD_DIM), lambda bi, g, i: (bi, 0, g)),
            pl.BlockSpec((None, None, s // KV_CHUNK, VT_ROWS, KV_CHUNK),
                         lambda bi, g, i: (bi, g, 0, 0, 0)),
            pl.BlockSpec((META_PAD, HEAD_DIM), lambda bi, g, i: (0, g)),
            pl.BlockSpec((None, VT_ROWS, META_PAD), lambda bi, g, i: (g, 0, 0)),
        ],
        out_specs=pl.BlockSpec((None, Q_TILE, GQA_GROUP * HEAD_DIM), lambda bi, g, i: (bi, i, g)),
        out_shape=jax.ShapeDtypeStruct((b, s, ATTN_WIDTH), BF16),
        scratch_shapes=[
            pltpu.VMEM((2, GQA_GROUP, KV_CHUNK, Q_TILE), F32),
            pltpu.VMEM((2, GQA_GROUP, 1, Q_TILE), F32),
            pltpu.VMEM((GQA_GROUP, 1, Q_TILE), F32),
            pltpu.VMEM((GQA_GROUP, VT_ROWS, Q_TILE), F32),
        ],
        compiler_params=pltpu.CompilerParams(
            dimension_semantics=("arbitrary", "arbitrary", "arbitrary"),
            vmem_limit_bytes=_vmem_limit(vmem)),
        name="attention_bounded" if bounded else "attention",
    )(q, k, vt, k_meta, vt_meta)


def _pool_fold_kernel(wg_ref, ps_ref, wp_ref, o_ref):
    for gi in range(N_POOL_GROUPS):
        rows = slice(gi * POOL_GROUP_DIM, (gi + 1) * POOL_GROUP_DIM)
        scaled = wg_ref[gi] * ps_ref[:, rows]
        o_ref[rows, :] = jnp.dot(scaled, wp_ref[rows, :], preferred_element_type=F32,
                                 precision=lax.Precision.HIGHEST).astype(BF16)


def _pool_fold(w_pool_grp, pool_scale, w_pool_br):
    d = w_pool_br.shape[1]
    return pl.pallas_call(
        _pool_fold_kernel,
        grid=(1,),
        in_specs=[
            pl.BlockSpec((N_POOL_GROUPS, POOL_GROUP_DIM, POOL_GROUP_DIM), lambda i: (0, 0, 0)),
            pl.BlockSpec((1, POOL_WIDTH), lambda i: (0, 0)),
            pl.BlockSpec((POOL_WIDTH, d), lambda i: (0, 0)),
        ],
        out_specs=pl.BlockSpec((POOL_WIDTH, d), lambda i: (0, 0)),
        out_shape=jax.ShapeDtypeStruct((POOL_WIDTH, d), BF16),
        compiler_params=pltpu.CompilerParams(dimension_semantics=("arbitrary",)),
        name="pool_fold",
    )(w_pool_grp, pool_scale, w_pool_br)


def _mlp_rows(x, g1_ref, w1_ref, w2_ref, g2_ref):
    h = (x * _rms_scale(x) * g1_ref[...]).astype(BF16)
    z = None
    for f in range(D_FF // FF_CHUNK):
        cols = slice(f * FF_CHUNK, (f + 1) * FF_CHUNK)
        u = jnp.dot(h, w1_ref[:, cols], preferred_element_type=F32)
        u = jnp.square(jnp.maximum(u, 0.0)).astype(BF16)
        zf = jnp.dot(u, w2_ref[cols, :], preferred_element_type=F32)
        z = zf if z is None else z + zf
    return x + z * _rms_scale(z) * g2_ref[...]


def _post_mix_kernel(attn_ref, p_ref, prev_ref, next_ref, pm_ref, gate_ref, x_ref,
                     wa_ref, wp_ref, wo_ref, g_ref, o_ref, ext_sc):
    tile = x_ref.shape[0]
    i = pl.program_id(1)
    nt = pl.num_programs(1)
    sub = tile // MIX_SPLIT
    ext = sub + 2 * POOL_HALO

    ext_sc[0:POOL_HALO, :] = jnp.where(i == 0, pm_ref[...], prev_ref[...])
    ext_sc[POOL_HALO:POOL_HALO + tile, :] = p_ref[...]
    ext_sc[POOL_HALO + tile:, :] = jnp.where(i == nt - 1, 0.0, next_ref[...])

    def ahead(a, k):
        return pltpu.roll(a, ext - k, 0)

    tail_row = lax.broadcasted_iota(jnp.int32, (POOL_HALO, POOL_GROUP_DIM), 0)

    for part in range(MIX_SPLIT):
        rows = slice(part * sub, (part + 1) * sub)
        attn_br = jnp.dot(attn_ref[rows, :], wa_ref[...], preferred_element_type=F32)

        pooled = []
        for gi, w in enumerate(POOL_WINDOWS):
            cols = slice(gi * POOL_GROUP_DIM, (gi + 1) * POOL_GROUP_DIM)
            half = w // 2
            e = ext_sc[part * sub:part * sub + ext, cols]
            run, span = e, 1
            while span < half:
                run = run + ahead(run, span)
                span *= 2
            if half == POOL_HALO:
                total = run[0:sub] + run[POOL_HALO:POOL_HALO + sub]
            else:
                total = (run + pltpu.roll(run, half, 0))[POOL_HALO:POOL_HALO + sub]
            mean = total * (1.0 / w)
            if part == MIX_SPLIT - 1:
                over = jnp.maximum(tail_row + (half - POOL_HALO), 0)
                tail = total[sub - POOL_HALO:] / (w - over).astype(F32)
                tail = jnp.where(i == nt - 1, tail, mean[sub - POOL_HALO:])
                mean = jnp.concatenate([mean[:sub - POOL_HALO], tail], axis=0)
            pooled.append((mean - p_ref[rows, cols]).astype(BF16))
        pool_br = jnp.dot(jnp.concatenate(pooled, axis=-1), wp_ref[...],
                          preferred_element_type=F32)

        gates = gate_ref[rows, :]
        mixed = (gates[:, :D_MODEL].astype(F32) * attn_br
                 + gates[:, D_MODEL:].astype(F32) * pool_br).astype(BF16)
        y = jnp.dot(mixed, wo_ref[...], preferred_element_type=F32)
        o_ref[rows, :] = x_ref[rows, :] + y * _rms_scale(y) * g_ref[...]


def _post_mix(attn, p_in, p_meta, gates, x, w_attn_br, w_pool, w_out, post_g):
    b, s, d = x.shape
    t = MIX_TILE
    nt = s // t
    hb = t // POOL_HALO
    last_hb = s // POOL_HALO - 1
    in_specs = [
        pl.BlockSpec((None, t, ATTN_WIDTH), lambda bi, i: (bi, i, 0)),
        pl.BlockSpec((None, t, POOL_WIDTH), lambda bi, i: (bi, i, 0)),
        pl.BlockSpec((None, POOL_HALO, POOL_WIDTH),
                     lambda bi, i: (bi, jnp.maximum(i * hb - 1, 0), 0)),
        pl.BlockSpec((None, POOL_HALO, POOL_WIDTH),
                     lambda bi, i: (bi, jnp.minimum((i + 1) * hb, last_hb), 0)),
        pl.BlockSpec((POOL_HALO, POOL_WIDTH), lambda bi, i: (N_META // POOL_HALO - 1, 0)),
        pl.BlockSpec((None, t, GATE_WIDTH), lambda bi, i: (bi, i, 0)),
        pl.BlockSpec((None, t, d), lambda bi, i: (bi, i, 0)),
        _const_spec((ATTN_WIDTH, d)),
        _const_spec((POOL_WIDTH, d)),
        _const_spec((d, d)),
        _const_spec((1, d)),
    ]
    vmem = (2 * t * (ATTN_WIDTH * 2 + POOL_WIDTH * 4 + GATE_WIDTH * 2 + 2 * d * 4)
            + (ATTN_WIDTH + POOL_WIDTH + d) * d * 2 + (t + 2 * POOL_HALO) * POOL_WIDTH * 4
            + 6 * (t // MIX_SPLIT) * d * 4)
    return pl.pallas_call(
        _post_mix_kernel,
        grid=(b, nt),
        in_specs=in_specs,
        out_specs=pl.BlockSpec((None, t, d), lambda bi, i: (bi, i, 0)),
        out_shape=jax.ShapeDtypeStruct((b, s, d), F32),
        scratch_shapes=[pltpu.VMEM((t + 2 * POOL_HALO, POOL_WIDTH), F32)],
        compiler_params=pltpu.CompilerParams(
            dimension_semantics=("arbitrary", "arbitrary"),
            vmem_limit_bytes=_vmem_limit(vmem)),
        name="post_mix",
    )(attn, p_in, p_in, p_in, p_meta, gates, x, w_attn_br, w_pool, w_out, post_g)


def _mlp_kernel(x_ref, g1_ref, w1_ref, w2_ref, g2_ref, o_ref):
    sub = x_ref.shape[0] // MLP_SPLIT
    for part in range(MLP_SPLIT):
        rows = slice(part * sub, (part + 1) * sub)
        o_ref[rows, :] = _mlp_rows(x_ref[rows, :], g1_ref, w1_ref, w2_ref, g2_ref)


def _mlp(x, pre_g, w1, w2, post_g):
    b, s, d = x.shape
    t = MLP_TILE
    sub = t // MLP_SPLIT
    vmem = 4 * t * d * 4 + 2 * d * D_FF * 2 + 4 * sub * FF_CHUNK * 4 + 2 * sub * d * 4
    return pl.pallas_call(
        _mlp_kernel,
        grid=(b, s // t),
        in_specs=[
            pl.BlockSpec((None, t, d), lambda bi, i: (bi, i, 0)),
            _const_spec((1, d)),
            _const_spec((d, D_FF)),
            _const_spec((D_FF, d)),
            _const_spec((1, d)),
        ],
        out_specs=pl.BlockSpec((None, t, d), lambda bi, i: (bi, i, 0)),
        out_shape=jax.ShapeDtypeStruct((b, s, d), F32),
        compiler_params=pltpu.CompilerParams(
            dimension_semantics=("arbitrary", "arbitrary"),
            vmem_limit_bytes=_vmem_limit(vmem)),
        name="mlp",
    )(x, pre_g, w1, w2, post_g)


def _rope_tables(seq):
    quarter = HEAD_DIM // 4
    inv_freq = ROPE_THETA ** (-np.arange(quarter, dtype=np.float64) / quarter)
    pos = np.arange(seq)
    ang_r = (pos // GRID_W)[:, None] * inv_freq[None, :]
    ang_c = (pos % GRID_W)[:, None] * inv_freq[None, :]
    cos = np.concatenate([np.cos(ang_r), np.cos(ang_r), np.cos(ang_c), np.cos(ang_c)], -1)
    sin = np.concatenate([-np.sin(ang_r), np.sin(ang_r), -np.sin(ang_c), np.sin(ang_c)], -1)
    return jnp.asarray(cos, F32), jnp.asarray(sin, F32)


def kernel(x, meta_tokens, pre_mix_g, q_norm_g, k_norm_g, w_in, w_attn_br, w_pool_grp,
           pool_scale, w_pool_br, w_out, post_mix_g, pre_mlp_g, w_mlp_in, w_mlp_out, post_mlp_g):
    b, s, d = x.shape
    assert d == D_MODEL and s % max(IN_TILE, MIX_TILE, MLP_TILE) == 0
    assert (Q_TILE % IN_TILE == 0 or IN_TILE % Q_TILE == 0) and Q_TILE % (IN_TILE // IN_SPLIT) == 0
    assert (IN_TILE // IN_SPLIT) % KV_CHUNK == 0 and s % GRID_W == 0
    assert s % Q_TILE == 0 and KV_UNROLL % 2 == 0 and MIX_TILE % (MIX_SPLIT * BF16_ROWS) == 0
    assert pre_mix_g.shape[0] == 1

    w_in_b = w_in[0].astype(BF16)
    pre_g = pre_mix_g[0][None, :]
    qg = q_norm_g[0][None, :]
    kg = k_norm_g[0][None, :]
    cos_tab, sin_tab = _rope_tables(s)

    (q, k, vt, p_in, gates), (w_attn_b, w_out_b, w_mlp_in_b, w_mlp_out_b) = _in_proj(
        x, pre_g, w_in_b, qg, kg, cos_tab, sin_tab,
        (w_attn_br[0], w_out[0], w_mlp_in[0], w_mlp_out[0]))

    meta_pad = jnp.zeros((META_PAD, d), F32).at[:N_META].set(meta_tokens.astype(F32))
    k_meta, vt_meta, p_meta = _meta_proj(meta_pad, pre_g, w_in_b, kg)

    score_bound = HEAD_DIM * Q_SCALE * jnp.max(jnp.abs(qg)) * jnp.max(jnp.abs(kg))
    attn = lax.cond(
        score_bound <= SCORE_BOUND,
        functools.partial(_attention, bounded=True),
        functools.partial(_attention, bounded=False),
        q, k, vt, k_meta, vt_meta)

    w_pool = _pool_fold(w_pool_grp[0], pool_scale[0][None, :], w_pool_br[0])
    x1 = _post_mix(attn, p_in, p_meta, gates, x, w_attn_b, w_pool, w_out_b,
                   post_mix_g[0][None, :])

    return _mlp(x1, pre_mlp_g[0][None, :], w_mlp_in_b, w_mlp_out_b, post_mlp_g[0][None, :])
```

```python
import functools
import math

import jax
import jax.numpy as jnp
import numpy as np
from jax import lax
from jax.experimental import pallas as pl
from jax.experimental.pallas import tpu as pltpu

D_MODEL = 1024
N_META = 16
GRID_W = 64
HEAD_DIM = 128
N_Q_HEADS = 8
N_KV_HEADS = 2
GQA_GROUP = N_Q_HEADS // N_KV_HEADS
ATTN_WIDTH = N_Q_HEADS * HEAD_DIM
KV_WIDTH = N_KV_HEADS * HEAD_DIM
POOL_WINDOWS = (2, 4, 8, 16)
N_POOL_GROUPS = len(POOL_WINDOWS)
POOL_GROUP_DIM = 128
POOL_WIDTH = N_POOL_GROUPS * POOL_GROUP_DIM
GATE_WIDTH = 2 * D_MODEL
D_FF = 4 * D_MODEL
ROPE_THETA = 10000.0
NORM_EPS = 1e-6

Q_OFF = 0
K_OFF = Q_OFF + ATTN_WIDTH
V_OFF = K_OFF + KV_WIDTH
P_OFF = V_OFF + KV_WIDTH
G_OFF = P_OFF + POOL_WIDTH
IN_WIDTH = G_OFF + GATE_WIDTH

V7X_VMEM_BYTES = 64 * 1024 * 1024
SUBLANES = 8
LANES = 128
BF16_ROWS = 16
VT_ROWS = HEAD_DIM + BF16_ROWS
POOL_HALO = SUBLANES

IN_TILE = 1024
IN_SPLIT = 4
Q_TILE = 1024
KV_CHUNK = 256
KV_UNROLL = 10
META_PAD = KV_CHUNK
MIX_TILE = 1024
MIX_SPLIT = 4
MLP_TILE = 1024
MLP_SPLIT = 4
FF_CHUNK = 1024

Q_SCALE = math.log2(math.e) / math.sqrt(HEAD_DIM)
MASK_VALUE = -1e30
SCORE_BOUND = -1.0

F32 = jnp.float32
BF16 = jnp.bfloat16
NT_DIMS = (((1,), (1,)), ((), ()))


def _vmem_limit(nbytes):
    return int(min(V7X_VMEM_BYTES - (4 << 20), max(nbytes, 16 << 20)))


def _const_spec(shape):
    nd = len(shape)
    return pl.BlockSpec(shape, lambda *_: (0,) * nd, pipeline_mode=pl.Buffered(1))


def _rms_scale(t):
    return lax.rsqrt(jnp.mean(t * t, axis=-1, keepdims=True) + NORM_EPS)


def _norm_rope(t, gain, cos, sin, first_half):
    y = t * _rms_scale(t) * gain
    swapped = jnp.where(first_half, pltpu.roll(y, 96, 1), pltpu.roll(y, 32, 1))
    return y * cos + swapped * sin


def _in_proj_kernel(x_ref, g_ref, w_ref, qg_ref, kg_ref, cos_ref, sin_ref, *refs, n_cast):
    cast_in = refs[:n_cast]
    q_ref, k_ref, vt_ref, p_ref, gate_ref = refs[n_cast:n_cast + 5]
    for src, dst in zip(cast_in, refs[n_cast + 5:]):
        dst[...] = src[...].astype(BF16)

    tile = x_ref.shape[0]
    sub = tile // IN_SPLIT
    lane = lax.broadcasted_iota(jnp.int32, (sub, HEAD_DIM), 1)
    first_half = (lane & 32) == 0
    qg = qg_ref[...] * Q_SCALE
    kg = kg_ref[...]

    for part in range(IN_SPLIT):
        r0 = part * sub
        rows = slice(r0, r0 + sub)
        x = x_ref[rows, :]
        h = (x * _rms_scale(x) * g_ref[...]).astype(BF16)
        cos = cos_ref[rows, :]
        sin = sin_ref[rows, :]

        qp = jnp.dot(h, w_ref[:, Q_OFF:K_OFF], preferred_element_type=F32)
        qt, qr = divmod(r0, Q_TILE)
        for hd in range(N_Q_HEADS):
            o = _norm_rope(qp[:, hd * HEAD_DIM:(hd + 1) * HEAD_DIM], qg, cos, sin, first_half)
            grp, j = divmod(hd, GQA_GROUP)
            q_ref[grp, qt, j, qr:qr + sub, :] = o.astype(BF16)

        kp = jnp.dot(h, w_ref[:, K_OFF:V_OFF], preferred_element_type=F32)
        for hd in range(N_KV_HEADS):
            o = _norm_rope(kp[:, hd * HEAD_DIM:(hd + 1) * HEAD_DIM], kg, cos, sin, first_half)
            k_ref[rows, hd * HEAD_DIM:(hd + 1) * HEAD_DIM] = o.astype(BF16)

        vp = jnp.dot(h, w_ref[:, V_OFF:P_OFF], preferred_element_type=F32)
        for hd in range(N_KV_HEADS):
            vt = vp[:, hd * HEAD_DIM:(hd + 1) * HEAD_DIM].T.astype(BF16)
            for c in range(sub // KV_CHUNK):
                chunk = r0 // KV_CHUNK + c
                vt_ref[hd, chunk, 0:HEAD_DIM, :] = vt[:, c * KV_CHUNK:(c + 1) * KV_CHUNK]
                vt_ref[hd, chunk, HEAD_DIM:, :] = jnp.ones((BF16_ROWS, KV_CHUNK), BF16)

        p_ref[rows, :] = jnp.dot(h, w_ref[:, P_OFF:G_OFF], preferred_element_type=F32)

        gl = jnp.dot(h, w_ref[:, G_OFF:IN_WIDTH], preferred_element_type=F32)
        gate_ref[rows, :] = jax.nn.sigmoid(gl).astype(BF16)


def _in_proj(x, pre_g, w_in, qg, kg, cos_tab, sin_tab, cast_weights):
    b, s, d = x.shape
    t = IN_TILE
    nt = s // t
    steps = b * nt
    cast_specs = []
    for w in cast_weights:
        rows = w.shape[0] // steps
        assert w.shape[0] == rows * steps and rows % BF16_ROWS == 0
        cast_specs.append(pl.BlockSpec((rows, w.shape[1]), lambda bi, i: (bi * nt + i, 0)))
    q_tiles = max(1, t // Q_TILE)
    q_rows = min(t, Q_TILE)
    out_shape = (
        jax.ShapeDtypeStruct((b, N_KV_HEADS, s // Q_TILE, GQA_GROUP, Q_TILE, HEAD_DIM), BF16),
        jax.ShapeDtypeStruct((b, s, KV_WIDTH), BF16),
        jax.ShapeDtypeStruct((b, N_KV_HEADS, s // KV_CHUNK, VT_ROWS, KV_CHUNK), BF16),
        jax.ShapeDtypeStruct((b, s, POOL_WIDTH), F32),
        jax.ShapeDtypeStruct((b, s, GATE_WIDTH), BF16),
    )
    in_specs = [
        pl.BlockSpec((None, t, d), lambda bi, i: (bi, i, 0)),
        _const_spec((1, d)),
        _const_spec((d, IN_WIDTH)),
        _const_spec((1, HEAD_DIM)),
        _const_spec((1, HEAD_DIM)),
        pl.BlockSpec((t, HEAD_DIM), lambda bi, i: (i, 0)),
        pl.BlockSpec((t, HEAD_DIM), lambda bi, i: (i, 0)),
    ] + cast_specs
    out_shape = out_shape + tuple(jax.ShapeDtypeStruct(w.shape, BF16) for w in cast_weights)
    out_specs = (
        pl.BlockSpec((None, N_KV_HEADS, q_tiles, GQA_GROUP, q_rows, HEAD_DIM),
                     lambda bi, i: (bi, 0, (i * t) // (q_tiles * Q_TILE), 0,
                                    ((i * t) % Q_TILE) // q_rows, 0)),
        pl.BlockSpec((None, t, KV_WIDTH), lambda bi, i: (bi, i, 0)),
        pl.BlockSpec((None, N_KV_HEADS, t // KV_CHUNK, VT_ROWS, KV_CHUNK),
                     lambda bi, i: (bi, 0, i, 0, 0)),
        pl.BlockSpec((None, t, POOL_WIDTH), lambda bi, i: (bi, i, 0)),
        pl.BlockSpec((None, t, GATE_WIDTH), lambda bi, i: (bi, i, 0)),
    ) + tuple(cast_specs)
    vmem = (2 * t * d * 4 + d * IN_WIDTH * 2
            + 2 * t * (ATTN_WIDTH * 2 + 2 * KV_WIDTH * 2 + POOL_WIDTH * 4 + GATE_WIDTH * 2)
            + 4 * (t // IN_SPLIT) * GATE_WIDTH * 4
            + 2 * sum(w.size * 6 for w in cast_weights) // steps)
    outs = pl.pallas_call(
        functools.partial(_in_proj_kernel, n_cast=len(cast_weights)),
        grid=(b, nt),
        in_specs=in_specs,
        out_specs=out_specs,
        out_shape=out_shape,
        compiler_params=pltpu.CompilerParams(
            dimension_semantics=("arbitrary", "arbitrary"),
            vmem_limit_bytes=_vmem_limit(vmem)),
        name="in_proj",
    )(x, pre_g, w_in, qg, kg, cos_tab, sin_tab, *cast_weights)
    return outs[:5], outs[5:]


def _meta_proj_kernel(x_ref, g_ref, w_ref, kg_ref, k_ref, vt_ref, p_ref):
    x = x_ref[...]
    h = (x * _rms_scale(x) * g_ref[...]).astype(BF16)
    proj = jnp.dot(h, w_ref[...], preferred_element_type=F32)
    kg = kg_ref[...]
    for hd in range(N_KV_HEADS):
        t = proj[:, hd * HEAD_DIM:(hd + 1) * HEAD_DIM]
        k_ref[:, hd * HEAD_DIM:(hd + 1) * HEAD_DIM] = (t * _rms_scale(t) * kg).astype(BF16)
        v = proj[:, KV_WIDTH + hd * HEAD_DIM:KV_WIDTH + (hd + 1) * HEAD_DIM]
        vt_ref[hd, 0:HEAD_DIM, :] = v.T.astype(BF16)
        col = lax.broadcasted_iota(jnp.int32, (BF16_ROWS, META_PAD), 1)
        vt_ref[hd, HEAD_DIM:, :] = jnp.where(col < N_META, 1.0, 0.0).astype(BF16)
    p_ref[...] = proj[:, 2 * KV_WIDTH:]


def _meta_proj(meta_pad, pre_g, w_in, kg):
    width = G_OFF - K_OFF
    assert width == D_MODEL and K_OFF == width
    return pl.pallas_call(
        _meta_proj_kernel,
        grid=(1,),
        in_specs=[
            pl.BlockSpec((META_PAD, D_MODEL), lambda i: (0, 0)),
            pl.BlockSpec((1, D_MODEL), lambda i: (0, 0)),
            pl.BlockSpec((D_MODEL, width), lambda i: (0, 1)),
            pl.BlockSpec((1, HEAD_DIM), lambda i: (0, 0)),
        ],
        out_specs=(
            pl.BlockSpec((META_PAD, KV_WIDTH), lambda i: (0, 0)),
            pl.BlockSpec((N_KV_HEADS, VT_ROWS, META_PAD), lambda i: (0, 0, 0)),
            pl.BlockSpec((META_PAD, POOL_WIDTH), lambda i: (0, 0)),
        ),
        out_shape=(
            jax.ShapeDtypeStruct((META_PAD, KV_WIDTH), BF16),
            jax.ShapeDtypeStruct((N_KV_HEADS, VT_ROWS, META_PAD), BF16),
            jax.ShapeDtypeStruct((META_PAD, POOL_WIDTH), F32),
        ),
        compiler_params=pltpu.CompilerParams(dimension_semantics=("arbitrary",)),
        name="meta_proj",
    )(meta_pad, pre_g, w_in, kg)


def _attn_kernel(q_ref, k_ref, vt_ref, km_ref, vtm_ref, o_ref, s_sc, cmax_sc, m_sc, acc_sc, *,
                 bounded):
    n_real = k_ref.shape[0] // KV_CHUNK

    def q_head(j):
        return q_ref[j]

    def produce(j, c, slot):
        if isinstance(c, int) and c == n_real:
            kc = km_ref[...]
        elif isinstance(c, int):
            kc = k_ref[c * KV_CHUNK:(c + 1) * KV_CHUNK, :]
        else:
            kc = k_ref[pl.ds(pl.multiple_of(c * KV_CHUNK, KV_CHUNK), KV_CHUNK), :]
        s = lax.dot_general(kc, q_head(j), NT_DIMS, preferred_element_type=F32)
        if isinstance(c, int) and c == n_real:
            row = lax.broadcasted_iota(jnp.int32, s.shape, 0)
            s = jnp.where(row < N_META, s, MASK_VALUE)
        s_sc[slot, j] = s
        if not bounded:
            cmax_sc[slot, j] = jnp.max(s, axis=0, keepdims=True)

    def consume(j, c, slot):
        vt = vtm_ref[...] if isinstance(c, int) and c == n_real else vt_ref[c]
        if bounded:
            p = jnp.exp2(s_sc[slot, j]).astype(BF16)
            acc_sc[j] += jnp.dot(vt, p, preferred_element_type=F32)
            return
        m_prev = m_sc[j]
        m_new = jnp.maximum(m_prev, cmax_sc[slot, j])
        alpha = jnp.exp2(m_prev - m_new)
        p = jnp.exp2(s_sc[slot, j] - m_new).astype(BF16)
        pv = jnp.dot(vt, p, preferred_element_type=F32)
        acc_sc[j] = alpha * acc_sc[j] + pv
        m_sc[j] = m_new

    for j in range(GQA_GROUP):
        if not bounded:
            m_sc[j] = jnp.full((1, Q_TILE), MASK_VALUE, F32)
        acc_sc[j] = jnp.zeros((VT_ROWS, Q_TILE), F32)
        produce(j, 0, 0)

    def steps(c0, count):
        for u in range(count):
            for j in range(GQA_GROUP):
                produce(j, c0 + u + 1, (u + 1) % 2)
                consume(j, c0 + u, u % 2)

    def body(i, carry):
        steps(i * KV_UNROLL, KV_UNROLL)
        return carry

    n_loop = n_real // KV_UNROLL
    lax.fori_loop(0, n_loop, body, 0)
    done = n_loop * KV_UNROLL
    steps(done, n_real - done)
    for j in range(GQA_GROUP):
        consume(j, n_real, n_real % 2)
        acc = acc_sc[j]
        out = acc[:HEAD_DIM] / acc[HEAD_DIM:HEAD_DIM + 1]
        o_ref[:, j * HEAD_DIM:(j + 1) * HEAD_DIM] = out.T.astype(BF16)


def _attention(q, k, vt, k_meta, vt_meta, *, bounded):
    b, _, nq, _, _, _ = q.shape
    s = k.shape[1]
    vmem = (2 * (s * HEAD_DIM * 2 + s * VT_ROWS * 2 + 2 * Q_TILE * GQA_GROUP * HEAD_DIM * 2)
            + GQA_GROUP * Q_TILE * 4 * (2 * KV_CHUNK + VT_ROWS + 3 * SUBLANES)
            + 8 * KV_CHUNK * Q_TILE * 4)
    return pl.pallas_call(
        functools.partial(_attn_kernel, bounded=bounded),
        grid=(b, N_KV_HEADS, nq),
        in_specs=[
            pl.BlockSpec((None, None, None, GQA_GROUP, Q_TILE, HEAD_DIM),
                         lambda bi, g, i: (bi, g, i, 0, 0, 0)),
            pl.BlockSpec((None, s, HEAD_DIM), lambda bi, g, i: (bi, 0, g)),
            pl.BlockSpec((None, None, s // KV_CHUNK, VT_ROWS, KV_CHUNK),
                         lambda bi, g, i: (bi, g, 0, 0, 0)),
            pl.BlockSpec((META_PAD, HEAD_DIM), lambda bi, g, i: (0, g)),
            pl.BlockSpec((None, VT_ROWS, META_PAD), lambda bi, g, i: (g, 0, 0)),
        ],
        out_specs=pl.BlockSpec((None, Q_TILE, GQA_GROUP * HEAD_DIM), lambda bi, g, i: (bi, i, g)),
        out_shape=jax.ShapeDtypeStruct((b, s, ATTN_WIDTH), BF16),
        scratch_shapes=[
            pltpu.VMEM((2, GQA_GROUP, KV_CHUNK, Q_TILE), F32),
            pltpu.VMEM((2, GQA_GROUP, 1, Q_TILE), F32),
            pltpu.VMEM((GQA_GROUP, 1, Q_TILE), F32),
            pltpu.VMEM((GQA_GROUP, VT_ROWS, Q_TILE), F32),
        ],
        compiler_params=pltpu.CompilerParams(
            dimension_semantics=("arbitrary", "arbitrary", "arbitrary"),
            vmem_limit_bytes=_vmem_limit(vmem)),
        name="attention_bounded" if bounded else "attention",
    )(q, k, vt, k_meta, vt_meta)


def _pool_fold_kernel(wg_ref, ps_ref, wp_ref, o_ref):
    for gi in range(N_POOL_GROUPS):
        rows = slice(gi * POOL_GROUP_DIM, (gi + 1) * POOL_GROUP_DIM)
        scaled = wg_ref[gi] * ps_ref[:, rows]
        o_ref[rows, :] = jnp.dot(scaled, wp_ref[rows, :], preferred_element_type=F32,
                                 precision=lax.Precision.HIGHEST).astype(BF16)


def _pool_fold(w_pool_grp, pool_scale, w_pool_br):
    d = w_pool_br.shape[1]
    return pl.pallas_call(
        _pool_fold_kernel,
        grid=(1,),
        in_specs=[
            pl.BlockSpec((N_POOL_GROUPS, POOL_GROUP_DIM, POOL_GROUP_DIM), lambda i: (0, 0, 0)),
            pl.BlockSpec((1, POOL_WIDTH), lambda i: (0, 0)),
            pl.BlockSpec((POOL_WIDTH, d), lambda i: (0, 0)),
        ],
        out_specs=pl.BlockSpec((POOL_WIDTH, d), lambda i: (0, 0)),
        out_shape=jax.ShapeDtypeStruct((POOL_WIDTH, d), BF16),
        compiler_params=pltpu.CompilerParams(dimension_semantics=("arbitrary",)),
        name="pool_fold",
    )(w_pool_grp, pool_scale, w_pool_br)


def _mlp_rows(x, g1_ref, w1_ref, w2_ref, g2_ref):
    h = (x * _rms_scale(x) * g1_ref[...]).astype(BF16)
    z = None
    for f in range(D_FF // FF_CHUNK):
        cols = slice(f * FF_CHUNK, (f + 1) * FF_CHUNK)
        u = jnp.dot(h, w1_ref[:, cols], preferred_element_type=F32)
        u = jnp.square(jnp.maximum(u, 0.0)).astype(BF16)
        zf = jnp.dot(u, w2_ref[cols, :], preferred_element_type=F32)
        z = zf if z is None else z + zf
    return x + z * _rms_scale(z) * g2_ref[...]


def _post_mix_kernel(attn_ref, p_ref, prev_ref, next_ref, pm_ref, gate_ref, x_ref,
                     wa_ref, wp_ref, wo_ref, g_ref, o_ref, ext_sc):
    tile = x_ref.shape[0]
    i = pl.program_id(1)
    nt = pl.num_programs(1)
    sub = tile // MIX_SPLIT
    ext = sub + 2 * POOL_HALO

    ext_sc[0:POOL_HALO, :] = jnp.where(i == 0, pm_ref[...], prev_ref[...])
    ext_sc[POOL_HALO:POOL_HALO + tile, :] = p_ref[...]
    ext_sc[POOL_HALO + tile:, :] = jnp.where(i == nt - 1, 0.0, next_ref[...])

    def ahead(a, k):
        return pltpu.roll(a, ext - k, 0)

    tail_row = lax.broadcasted_iota(jnp.int32, (POOL_HALO, POOL_GROUP_DIM), 0)

    for part in range(MIX_SPLIT):
        rows = slice(part * sub, (part + 1) * sub)
        attn_br = jnp.dot(attn_ref[rows, :], wa_ref[...], preferred_element_type=F32)

        pooled = []
        for gi, w in enumerate(POOL_WINDOWS):
            cols = slice(gi * POOL_GROUP_DIM, (gi + 1) * POOL_GROUP_DIM)
            half = w // 2
            e = ext_sc[part * sub:part * sub + ext, cols]
            run, span = e, 1
            while span < half:
                run = run + ahead(run, span)
                span *= 2
            if half == POOL_HALO:
                total = run[0:sub] + run[POOL_HALO:POOL_HALO + sub]
            else:
                total = (run + pltpu.roll(run, half, 0))[POOL_HALO:POOL_HALO + sub]
            mean = total * (1.0 / w)
            if part == MIX_SPLIT - 1:
                over = jnp.maximum(tail_row + (half - POOL_HALO), 0)
                tail = total[sub - POOL_HALO:] / (w - over).astype(F32)
                tail = jnp.where(i == nt - 1, tail, mean[sub - POOL_HALO:])
                mean = jnp.concatenate([mean[:sub - POOL_HALO], tail], axis=0)
            pooled.append((mean - p_ref[rows, cols]).astype(BF16))
        pool_br = jnp.dot(jnp.concatenate(pooled, axis=-1), wp_ref[...],
                          preferred_element_type=F32)

        gates = gate_ref[rows, :]
        mixed = (gates[:, :D_MODEL].astype(F32) * attn_br
                 + gates[:, D_MODEL:].astype(F32) * pool_br).astype(BF16)
        y = jnp.dot(mixed, wo_ref[...], preferred_element_type=F32)
        o_ref[rows, :] = x_ref[rows, :] + y * _rms_scale(y) * g_ref[...]


def _post_mix(attn, p_in, p_meta, gates, x, w_attn_br, w_pool, w_out, post_g):
    b, s, d = x.shape
    t = MIX_TILE
    nt = s // t
    hb = t // POOL_HALO
    last_hb = s // POOL_HALO - 1
    in_specs = [
        pl.BlockSpec((None, t, ATTN_WIDTH), lambda bi, i: (bi, i, 0)),
        pl.BlockSpec((None, t, POOL_WIDTH), lambda bi, i: (bi, i, 0)),
        pl.BlockSpec((None, POOL_HALO, POOL_WIDTH),
                     lambda bi, i: (bi, jnp.maximum(i * hb - 1, 0), 0)),
        pl.BlockSpec((None, POOL_HALO, POOL_WIDTH),
                     lambda bi, i: (bi, jnp.minimum((i + 1) * hb, last_hb), 0)),
        pl.BlockSpec((POOL_HALO, POOL_WIDTH), lambda bi, i: (N_META // POOL_HALO - 1, 0)),
        pl.BlockSpec((None, t, GATE_WIDTH), lambda bi, i: (bi, i, 0)),
        pl.BlockSpec((None, t, d), lambda bi, i: (bi, i, 0)),
        _const_spec((ATTN_WIDTH, d)),
        _const_spec((POOL_WIDTH, d)),
        _const_spec((d, d)),
        _const_spec((1, d)),
    ]
    vmem = (2 * t * (ATTN_WIDTH * 2 + POOL_WIDTH * 4 + GATE_WIDTH * 2 + 2 * d * 4)
            + (ATTN_WIDTH + POOL_WIDTH + d) * d * 2 + (t + 2 * POOL_HALO) * POOL_WIDTH * 4
            + 6 * (t // MIX_SPLIT) * d * 4)
    return pl.pallas_call(
        _post_mix_kernel,
        grid=(b, nt),
        in_specs=in_specs,
        out_specs=pl.BlockSpec((None, t, d), lambda bi, i: (bi, i, 0)),
        out_shape=jax.ShapeDtypeStruct((b, s, d), F32),
        scratch_shapes=[pltpu.VMEM((t + 2 * POOL_HALO, POOL_WIDTH), F32)],
        compiler_params=pltpu.CompilerParams(
            dimension_semantics=("arbitrary", "arbitrary"),
            vmem_limit_bytes=_vmem_limit(vmem)),
        name="post_mix",
    )(attn, p_in, p_in, p_in, p_meta, gates, x, w_attn_br, w_pool, w_out, post_g)


def _mlp_kernel(x_ref, g1_ref, w1_ref, w2_ref, g2_ref, o_ref):
    sub = x_ref.shape[0] // MLP_SPLIT
    for part in range(MLP_SPLIT):
        rows = slice(part * sub, (part + 1) * sub)
        o_ref[rows, :] = _mlp_rows(x_ref[rows, :], g1_ref, w1_ref, w2_ref, g2_ref)


def _mlp(x, pre_g, w1, w2, post_g):
    b, s, d = x.shape
    t = MLP_TILE
    sub = t // MLP_SPLIT
    vmem = 4 * t * d * 4 + 2 * d * D_FF * 2 + 4 * sub * FF_CHUNK * 4 + 2 * sub * d * 4
    return pl.pallas_call(
        _mlp_kernel,
        grid=(b, s // t),
        in_specs=[
            pl.BlockSpec((None, t, d), lambda bi, i: (bi, i, 0)),
            _const_spec((1, d)),
            _const_spec((d, D_FF)),
            _const_spec((D_FF, d)),
            _const_spec((1, d)),
        ],
        out_specs=pl.BlockSpec((None, t, d), lambda bi, i: (bi, i, 0)),
        out_shape=jax.ShapeDtypeStruct((b, s, d), F32),
        compiler_params=pltpu.CompilerParams(
            dimension_semantics=("arbitrary", "arbitrary"),
            vmem_limit_bytes=_vmem_limit(vmem)),
        name="mlp",
    )(x, pre_g, w1, w2, post_g)


def _rope_tables(seq):
    quarter = HEAD_DIM // 4
    inv_freq = ROPE_THETA ** (-np.arange(quarter, dtype=np.float64) / quarter)
    pos = np.arange(seq)
    ang_r = (pos // GRID_W)[:, None] * inv_freq[None, :]
    ang_c = (pos % GRID_W)[:, None] * inv_freq[None, :]
    cos = np.concatenate([np.cos(ang_r), np.cos(ang_r), np.cos(ang_c), np.cos(ang_c)], -1)
    sin = np.concatenate([-np.sin(ang_r), np.sin(ang_r), -np.sin(ang_c), np.sin(ang_c)], -1)
    return jnp.asarray(cos, F32), jnp.asarray(sin, F32)


def kernel(x, meta_tokens, pre_mix_g, q_norm_g, k_norm_g, w_in, w_attn_br, w_pool_grp,
           pool_scale, w_pool_br, w_out, post_mix_g, pre_mlp_g, w_mlp_in, w_mlp_out, post_mlp_g):
    b, s, d = x.shape
    assert d == D_MODEL and s % max(IN_TILE, MIX_TILE, MLP_TILE) == 0
    assert (Q_TILE % IN_TILE == 0 or IN_TILE % Q_TILE == 0) and Q_TILE % (IN_TILE // IN_SPLIT) == 0
    assert (IN_TILE // IN_SPLIT) % KV_CHUNK == 0 and s % GRID_W == 0
    assert s % Q_TILE == 0 and KV_UNROLL % 2 == 0 and MIX_TILE % (MIX_SPLIT * BF16_ROWS) == 0
    assert pre_mix_g.shape[0] == 1

    w_in_b = w_in[0].astype(BF16)
    pre_g = pre_mix_g[0][None, :]
    qg = q_norm_g[0][None, :]
    kg = k_norm_g[0][None, :]
    cos_tab, sin_tab = _rope_tables(s)

    (q, k, vt, p_in, gates), (w_attn_b, w_out_b, w_mlp_in_b, w_mlp_out_b) = _in_proj(
        x, pre_g, w_in_b, qg, kg, cos_tab, sin_tab,
        (w_attn_br[0], w_out[0], w_mlp_in[0], w_mlp_out[0]))

    meta_pad = jnp.zeros((META_PAD, d), F32).at[:N_META].set(meta_tokens.astype(F32))
    k_meta, vt_meta, p_meta = _meta_proj(meta_pad, pre_g, w_in_b, kg)

    score_bound = HEAD_DIM * Q_SCALE * jnp.max(jnp.abs(qg)) * jnp.max(jnp.abs(kg))
    attn = lax.cond(
        score_bound <= SCORE_BOUND,
        functools.partial(_attention, bounded=True),
        functools.partial(_attention, bounded=False),
        q, k, vt, k_meta, vt_meta)

    w_pool = _pool_fold(w_pool_grp[0], pool_scale[0][None, :], w_pool_br[0])
    x1 = _post_mix(attn, p_in, p_meta, gates, x, w_attn_b, w_pool, w_out_b,
                   post_mix_g[0][None, :])

    return _mlp(x1, pre_mlp_g[0][None, :], w_mlp_in_b, w_mlp_out_b, post_mlp_g[0][None, :])
```

```python
import functools
import math

import jax
import jax.numpy as jnp
import numpy as np
from jax import lax
from jax.experimental import pallas as pl
from jax.experimental.pallas import tpu as pltpu

D_MODEL = 1024
N_META = 16
GRID_W = 64
HEAD_DIM = 128
N_Q_HEADS = 8
N_KV_HEADS = 2
GQA_GROUP = N_Q_HEADS // N_KV_HEADS
ATTN_WIDTH = N_Q_HEADS * HEAD_DIM
KV_WIDTH = N_KV_HEADS * HEAD_DIM
POOL_WINDOWS = (2, 4, 8, 16)
N_POOL_GROUPS = len(POOL_WINDOWS)
POOL_GROUP_DIM = 128
POOL_WIDTH = N_POOL_GROUPS * POOL_GROUP_DIM
GATE_WIDTH = 2 * D_MODEL
D_FF = 4 * D_MODEL
ROPE_THETA = 10000.0
ROPE_QUARTER = HEAD_DIM // 4
NORM_EPS = 1e-6

Q_OFF = 0
K_OFF = Q_OFF + ATTN_WIDTH
V_OFF = K_OFF + KV_WIDTH
P_OFF = V_OFF + KV_WIDTH
G_OFF = P_OFF + POOL_WIDTH
IN_WIDTH = G_OFF + GATE_WIDTH

V7X_VMEM_BYTES = 64 * 1024 * 1024
VMEM_RESERVED_BYTES = 4 * 1024 * 1024
VMEM_MIN_REQUEST_BYTES = 16 * 1024 * 1024
SUBLANES = 8
LANES = 128
BF16_ROWS = 16
VT_ROWS = HEAD_DIM + BF16_ROWS
POOL_HALO = SUBLANES

IN_TILE = 1024
IN_SPLIT = 4
Q_TILE = 1024
KV_CHUNK = 256
KV_UNROLL = 10
META_PAD = KV_CHUNK
MIX_TILE = 1024
MIX_SPLIT = 4
MLP_TILE = 1024
MLP_SPLIT = 4
FF_CHUNK = 1024

Q_SCALE = math.log2(math.e) / math.sqrt(HEAD_DIM)
MASK_VALUE = -1e30
SCORE_BOUND = 60.0

F32 = jnp.float32
BF16 = jnp.bfloat16
NT_DIMS = (((1,), (1,)), ((), ()))


def _vmem_limit(nbytes):
    return int(min(V7X_VMEM_BYTES - VMEM_RESERVED_BYTES, max(nbytes, VMEM_MIN_REQUEST_BYTES)))


def _const_spec(shape):
    nd = len(shape)
    return pl.BlockSpec(shape, lambda *_: (0,) * nd, pipeline_mode=pl.Buffered(1))


def _rms_scale(t):
    return lax.rsqrt(jnp.mean(t * t, axis=-1, keepdims=True) + NORM_EPS)


def _norm_rope(t, gain, cos, sin, first_half):
    y = t * _rms_scale(t) * gain
    swapped = jnp.where(first_half, pltpu.roll(y, HEAD_DIM - ROPE_QUARTER, 1),
                        pltpu.roll(y, ROPE_QUARTER, 1))
    return y * cos + swapped * sin


def _in_proj_kernel(x_ref, g_ref, w_ref, qg_ref, kg_ref, cos_ref, sin_ref, *refs, n_cast):
    cast_in = refs[:n_cast]
    q_ref, k_ref, vt_ref, p_ref, gate_ref = refs[n_cast:n_cast + 5]
    for src, dst in zip(cast_in, refs[n_cast + 5:]):
        dst[...] = src[...].astype(BF16)

    tile = x_ref.shape[0]
    sub = tile // IN_SPLIT
    lane = lax.broadcasted_iota(jnp.int32, (sub, HEAD_DIM), 1)
    first_half = (lane & ROPE_QUARTER) == 0
    qg = qg_ref[...] * Q_SCALE
    kg = kg_ref[...]

    for part in range(IN_SPLIT):
        r0 = part * sub
        rows = slice(r0, r0 + sub)
        x = x_ref[rows, :]
        h = (x * _rms_scale(x) * g_ref[...]).astype(BF16)
        cos = cos_ref[rows, :]
        sin = sin_ref[rows, :]

        qp = jnp.dot(h, w_ref[:, Q_OFF:K_OFF], preferred_element_type=F32)
        qt, qr = divmod(r0, Q_TILE)
        for hd in range(N_Q_HEADS):
            o = _norm_rope(qp[:, hd * HEAD_DIM:(hd + 1) * HEAD_DIM], qg, cos, sin, first_half)
            grp, j = divmod(hd, GQA_GROUP)
            q_ref[grp, qt, j, qr:qr + sub, :] = o.astype(BF16)

        kp = jnp.dot(h, w_ref[:, K_OFF:V_OFF], preferred_element_type=F32)
        for hd in range(N_KV_HEADS):
            o = _norm_rope(kp[:, hd * HEAD_DIM:(hd + 1) * HEAD_DIM], kg, cos, sin, first_half)
            k_ref[rows, hd * HEAD_DIM:(hd + 1) * HEAD_DIM] = o.astype(BF16)

        vp = jnp.dot(h, w_ref[:, V_OFF:P_OFF], preferred_element_type=F32)
        for hd in range(N_KV_HEADS):
            vt = vp[:, hd * HEAD_DIM:(hd + 1) * HEAD_DIM].T.astype(BF16)
            for c in range(sub // KV_CHUNK):
                chunk = r0 // KV_CHUNK + c
                vt_ref[hd, chunk, 0:HEAD_DIM, :] = vt[:, c * KV_CHUNK:(c + 1) * KV_CHUNK]
                vt_ref[hd, chunk, HEAD_DIM:, :] = jnp.ones((BF16_ROWS, KV_CHUNK), BF16)

        p_ref[rows, :] = jnp.dot(h, w_ref[:, P_OFF:G_OFF], preferred_element_type=F32)

        gl = jnp.dot(h, w_ref[:, G_OFF:IN_WIDTH], preferred_element_type=F32)
        gate_ref[rows, :] = jax.nn.sigmoid(gl).astype(BF16)


def _in_proj(x, pre_g, w_in, qg, kg, cos_tab, sin_tab, cast_weights):
    b, s, d = x.shape
    t = IN_TILE
    nt = s // t
    steps = b * nt
    cast_specs = []
    for w in cast_weights:
        rows = w.shape[0] // steps
        assert w.shape[0] == rows * steps and rows % BF16_ROWS == 0
        cast_specs.append(pl.BlockSpec((rows, w.shape[1]), lambda bi, i: (bi * nt + i, 0)))
    q_tiles = max(1, t // Q_TILE)
    q_rows = min(t, Q_TILE)
    out_shape = (
        jax.ShapeDtypeStruct((b, N_KV_HEADS, s // Q_TILE, GQA_GROUP, Q_TILE, HEAD_DIM), BF16),
        jax.ShapeDtypeStruct((b, s, KV_WIDTH), BF16),
        jax.ShapeDtypeStruct((b, N_KV_HEADS, s // KV_CHUNK, VT_ROWS, KV_CHUNK), BF16),
        jax.ShapeDtypeStruct((b, s, POOL_WIDTH), F32),
        jax.ShapeDtypeStruct((b, s, GATE_WIDTH), BF16),
    )
    in_specs = [
        pl.BlockSpec((None, t, d), lambda bi, i: (bi, i, 0)),
        _const_spec((1, d)),
        _const_spec((d, IN_WIDTH)),
        _const_spec((1, HEAD_DIM)),
        _const_spec((1, HEAD_DIM)),
        pl.BlockSpec((t, HEAD_DIM), lambda bi, i: (i, 0)),
        pl.BlockSpec((t, HEAD_DIM), lambda bi, i: (i, 0)),
    ] + cast_specs
    out_shape = out_shape + tuple(jax.ShapeDtypeStruct(w.shape, BF16) for w in cast_weights)
    out_specs = (
        pl.BlockSpec((None, N_KV_HEADS, q_tiles, GQA_GROUP, q_rows, HEAD_DIM),
                     lambda bi, i: (bi, 0, (i * t) // (q_tiles * Q_TILE), 0,
                                    ((i * t) % Q_TILE) // q_rows, 0)),
        pl.BlockSpec((None, t, KV_WIDTH), lambda bi, i: (bi, i, 0)),
        pl.BlockSpec((None, N_KV_HEADS, t // KV_CHUNK, VT_ROWS, KV_CHUNK),
                     lambda bi, i: (bi, 0, i, 0, 0)),
        pl.BlockSpec((None, t, POOL_WIDTH), lambda bi, i: (bi, i, 0)),
        pl.BlockSpec((None, t, GATE_WIDTH), lambda bi, i: (bi, i, 0)),
    ) + tuple(cast_specs)
    vmem = (2 * t * d * 4 + d * IN_WIDTH * 2
            + 2 * t * (ATTN_WIDTH * 2 + 2 * KV_WIDTH * 2 + POOL_WIDTH * 4 + GATE_WIDTH * 2)
            + 4 * (t // IN_SPLIT) * GATE_WIDTH * 4
            + 2 * sum(w.size * 6 for w in cast_weights) // steps)
    outs = pl.pallas_call(
        functools.partial(_in_proj_kernel, n_cast=len(cast_weights)),
        grid=(b, nt),
        in_specs=in_specs,
        out_specs=out_specs,
        out_shape=out_shape,
        compiler_params=pltpu.CompilerParams(
            dimension_semantics=("arbitrary", "arbitrary"),
            vmem_limit_bytes=_vmem_limit(vmem)),
        name="in_proj",
    )(x, pre_g, w_in, qg, kg, cos_tab, sin_tab, *cast_weights)
    return outs[:5], outs[5:]


def _meta_proj_kernel(x_ref, g_ref, w_ref, kg_ref, k_ref, vt_ref, p_ref):
    x = x_ref[...]
    h = (x * _rms_scale(x) * g_ref[...]).astype(BF16)
    proj = jnp.dot(h, w_ref[...], preferred_element_type=F32)
    kg = kg_ref[...]
    for hd in range(N_KV_HEADS):
        t = proj[:, hd * HEAD_DIM:(hd + 1) * HEAD_DIM]
        k_ref[:, hd * HEAD_DIM:(hd + 1) * HEAD_DIM] = (t * _rms_scale(t) * kg).astype(BF16)
        v = proj[:, KV_WIDTH + hd * HEAD_DIM:KV_WIDTH + (hd + 1) * HEAD_DIM]
        vt_ref[hd, 0:HEAD_DIM, :] = v.T.astype(BF16)
        col = lax.broadcasted_iota(jnp.int32, (BF16_ROWS, META_PAD), 1)
        vt_ref[hd, HEAD_DIM:, :] = jnp.where(col < N_META, 1.0, 0.0).astype(BF16)
    p_ref[...] = proj[:, 2 * KV_WIDTH:]


def _meta_proj(meta_pad, pre_g, w_in, kg):
    width = G_OFF - K_OFF
    assert width == D_MODEL and K_OFF == width
    return pl.pallas_call(
        _meta_proj_kernel,
        grid=(1,),
        in_specs=[
            pl.BlockSpec((META_PAD, D_MODEL), lambda i: (0, 0)),
            pl.BlockSpec((1, D_MODEL), lambda i: (0, 0)),
            pl.BlockSpec((D_MODEL, width), lambda i: (0, 1)),
            pl.BlockSpec((1, HEAD_DIM), lambda i: (0, 0)),
        ],
        out_specs=(
            pl.BlockSpec((META_PAD, KV_WIDTH), lambda i: (0, 0)),
            pl.BlockSpec((N_KV_HEADS, VT_ROWS, META_PAD), lambda i: (0, 0, 0)),
            pl.BlockSpec((META_PAD, POOL_WIDTH), lambda i: (0, 0)),
        ),
        out_shape=(
            jax.ShapeDtypeStruct((META_PAD, KV_WIDTH), BF16),
            jax.ShapeDtypeStruct((N_KV_HEADS, VT_ROWS, META_PAD), BF16),
            jax.ShapeDtypeStruct((META_PAD, POOL_WIDTH), F32),
        ),
        compiler_params=pltpu.CompilerParams(dimension_semantics=("arbitrary",)),
        name="meta_proj",
    )(meta_pad, pre_g, w_in, kg)


def _attn_kernel(q_ref, k_ref, vt_ref, km_ref, vtm_ref, o_ref, s_sc, acc_sc, *stats, bounded):
    if not bounded:
        cmax_sc, m_sc = stats
    n_real = k_ref.shape[0] // KV_CHUNK

    def q_head(j):
        return q_ref[j]

    def produce(j, c, slot):
        if isinstance(c, int) and c == n_real:
            kc = km_ref[...]
        elif isinstance(c, int):
            kc = k_ref[c * KV_CHUNK:(c + 1) * KV_CHUNK, :]
        else:
            kc = k_ref[pl.ds(pl.multiple_of(c * KV_CHUNK, KV_CHUNK), KV_CHUNK), :]
        s = lax.dot_general(kc, q_head(j), NT_DIMS, preferred_element_type=F32)
        if isinstance(c, int) and c == n_real:
            row = lax.broadcasted_iota(jnp.int32, s.shape, 0)
            s = jnp.where(row < N_META, s, MASK_VALUE)
        s_sc[slot, j] = s
        if not bounded:
            cmax_sc[slot, j] = jnp.max(s, axis=0, keepdims=True)

    def consume(j, c, slot):
        vt = vtm_ref[...] if isinstance(c, int) and c == n_real else vt_ref[c]
        if bounded:
            p = jnp.exp2(s_sc[slot, j]).astype(BF16)
            acc_sc[j] += jnp.dot(vt, p, preferred_element_type=F32)
            return
        m_prev = m_sc[j]
        m_new = jnp.maximum(m_prev, cmax_sc[slot, j])
        alpha = jnp.exp2(m_prev - m_new)
        p = jnp.exp2(s_sc[slot, j] - m_new).astype(BF16)
        pv = jnp.dot(vt, p, preferred_element_type=F32)
        acc_sc[j] = alpha * acc_sc[j] + pv
        m_sc[j] = m_new

    for j in range(GQA_GROUP):
        if not bounded:
            m_sc[j] = jnp.full((1, Q_TILE), MASK_VALUE, F32)
        acc_sc[j] = jnp.zeros((VT_ROWS, Q_TILE), F32)
        produce(j, 0, 0)

    def steps(c0, count):
        for u in range(count):
            for j in range(GQA_GROUP):
                produce(j, c0 + u + 1, (u + 1) % 2)
                consume(j, c0 + u, u % 2)

    def body(i, carry):
        steps(i * KV_UNROLL, KV_UNROLL)
        return carry

    n_loop = (n_real - 1) // KV_UNROLL
    lax.fori_loop(0, n_loop, body, 0)
    done = n_loop * KV_UNROLL
    steps(done, n_real - done)
    for j in range(GQA_GROUP):
        consume(j, n_real, n_real % 2)
        acc = acc_sc[j]
        out = acc[:HEAD_DIM] / acc[HEAD_DIM:HEAD_DIM + 1]
        o_ref[:, j * HEAD_DIM:(j + 1) * HEAD_DIM] = out.T.astype(BF16)


def _attention(q, k, vt, k_meta, vt_meta, *, bounded):
    b, _, nq, _, _, _ = q.shape
    s = k.shape[1]
    vmem = (2 * (s * HEAD_DIM * 2 + s * VT_ROWS * 2 + 2 * Q_TILE * GQA_GROUP * HEAD_DIM * 2)
            + GQA_GROUP * Q_TILE * 4 * (2 * KV_CHUNK + VT_ROWS + 3 * SUBLANES)
            + 8 * KV_CHUNK * Q_TILE * 4)
    return pl.pallas_call(
        functools.partial(_attn_kernel, bounded=bounded),
        grid=(b, N_KV_HEADS, nq),
        in_specs=[
            pl.BlockSpec((None, None, None, GQA_GROUP, Q_TILE, HEAD_DIM),
                         lambda bi, g, i: (bi, g, i, 0, 0, 0)),
            pl.BlockSpec((None, s, HEAD_DIM), lambda bi, g, i: (bi, 0, g)),
            pl.BlockSpec((None, None, s // KV_CHUNK, VT_ROWS, KV_CHUNK),
                         lambda bi, g, i: (bi, g, 0, 0, 0)),
            pl.BlockSpec((META_PAD, HEAD_DIM), lambda bi, g, i: (0, g)),
            pl.BlockSpec((None, VT_ROWS, META_PAD), lambda bi, g, i: (g, 0, 0)),
        ],
        out_specs=pl.BlockSpec((None, Q_TILE, GQA_GROUP * HEAD_DIM), lambda bi, g, i: (bi, i, g)),
        out_shape=jax.ShapeDtypeStruct((b, s, ATTN_WIDTH), BF16),
        scratch_shapes=[
            pltpu.VMEM((2, GQA_GROUP, KV_CHUNK, Q_TILE), F32),
            pltpu.VMEM((GQA_GROUP, VT_ROWS, Q_TILE), F32),
        ] + ([] if bounded else [
            pltpu.VMEM((2, GQA_GROUP, 1, Q_TILE), F32),
            pltpu.VMEM((GQA_GROUP, 1, Q_TILE), F32),
        ]),
        compiler_params=pltpu.CompilerParams(
            dimension_semantics=("arbitrary", "arbitrary", "arbitrary"),
            vmem_limit_bytes=_vmem_limit(vmem)),
        name="attention_bounded" if bounded else "attention",
    )(q, k, vt, k_meta, vt_meta)


def _pool_fold_kernel(wg_ref, ps_ref, wp_ref, o_ref):
    for gi in range(N_POOL_GROUPS):
        rows = slice(gi * POOL_GROUP_DIM, (gi + 1) * POOL_GROUP_DIM)
        scaled = wg_ref[gi] * ps_ref[:, rows]
        o_ref[rows, :] = jnp.dot(scaled, wp_ref[rows, :], preferred_element_type=F32,
                                 precision=lax.Precision.HIGHEST).astype(BF16)


def _pool_fold(w_pool_grp, pool_scale, w_pool_br):
    d = w_pool_br.shape[1]
    return pl.pallas_call(
        _pool_fold_kernel,
        grid=(1,),
        in_specs=[
            pl.BlockSpec((N_POOL_GROUPS, POOL_GROUP_DIM, POOL_GROUP_DIM), lambda i: (0, 0, 0)),
            pl.BlockSpec((1, POOL_WIDTH), lambda i: (0, 0)),
            pl.BlockSpec((POOL_WIDTH, d), lambda i: (0, 0)),
        ],
        out_specs=pl.BlockSpec((POOL_WIDTH, d), lambda i: (0, 0)),
        out_shape=jax.ShapeDtypeStruct((POOL_WIDTH, d), BF16),
        compiler_params=pltpu.CompilerParams(dimension_semantics=("arbitrary",)),
        name="pool_fold",
    )(w_pool_grp, pool_scale, w_pool_br)


def _mlp_rows(x, g1_ref, w1_ref, w2_ref, g2_ref):
    h = (x * _rms_scale(x) * g1_ref[...]).astype(BF16)
    z = None
    for f in range(D_FF // FF_CHUNK):
        cols = slice(f * FF_CHUNK, (f + 1) * FF_CHUNK)
        u = jnp.dot(h, w1_ref[:, cols], preferred_element_type=F32)
        u = jnp.square(jnp.maximum(u, 0.0)).astype(BF16)
        zf = jnp.dot(u, w2_ref[cols, :], preferred_element_type=F32)
        z = zf if z is None else z + zf
    return x + z * _rms_scale(z) * g2_ref[...]


def _post_mix_kernel(attn_ref, p_ref, prev_ref, next_ref, pm_ref, gate_ref, x_ref,
                     wa_ref, wp_ref, wo_ref, g_ref, o_ref, ext_sc):
    tile = x_ref.shape[0]
    i = pl.program_id(1)
    nt = pl.num_programs(1)
    sub = tile // MIX_SPLIT
    ext = sub + 2 * POOL_HALO

    ext_sc[0:POOL_HALO, :] = jnp.where(i == 0, pm_ref[...], prev_ref[...])
    ext_sc[POOL_HALO:POOL_HALO + tile, :] = p_ref[...]
    ext_sc[POOL_HALO + tile:, :] = jnp.where(i == nt - 1, 0.0, next_ref[...])

    def ahead(a, k):
        return pltpu.roll(a, ext - k, 0)

    tail_row = lax.broadcasted_iota(jnp.int32, (POOL_HALO, POOL_GROUP_DIM), 0)

    for part in range(MIX_SPLIT):
        rows = slice(part * sub, (part + 1) * sub)
        attn_br = jnp.dot(attn_ref[rows, :], wa_ref[...], preferred_element_type=F32)

        pooled = []
        for gi, w in enumerate(POOL_WINDOWS):
            cols = slice(gi * POOL_GROUP_DIM, (gi + 1) * POOL_GROUP_DIM)
            half = w // 2
            e = ext_sc[part * sub:part * sub + ext, cols]
            run, span = e, 1
            while span < half:
                run = run + ahead(run, span)
                span *= 2
            if half == POOL_HALO:
                total = run[0:sub] + run[POOL_HALO:POOL_HALO + sub]
            else:
                total = (run + pltpu.roll(run, half, 0))[POOL_HALO:POOL_HALO + sub]
            mean = total * (1.0 / w)
            if part == MIX_SPLIT - 1:
                over = jnp.maximum(tail_row + (half - POOL_HALO), 0)
                tail = total[sub - POOL_HALO:] / (w - over).astype(F32)
                tail = jnp.where(i == nt - 1, tail, mean[sub - POOL_HALO:])
                mean = jnp.concatenate([mean[:sub - POOL_HALO], tail], axis=0)
            pooled.append((mean - p_ref[rows, cols]).astype(BF16))
        pool_br = jnp.dot(jnp.concatenate(pooled, axis=-1), wp_ref[...],
                          preferred_element_type=F32)

        gates = gate_ref[rows, :]
        mixed = (gates[:, :D_MODEL].astype(F32) * attn_br
                 + gates[:, D_MODEL:].astype(F32) * pool_br).astype(BF16)
        y = jnp.dot(mixed, wo_ref[...], preferred_element_type=F32)
        o_ref[rows, :] = x_ref[rows, :] + y * _rms_scale(y) * g_ref[...]


def _post_mix(attn, p_in, p_meta, gates, x, w_attn_br, w_pool, w_out, post_g):
    b, s, d = x.shape
    t = MIX_TILE
    nt = s // t
    hb = t // POOL_HALO
    last_hb = s // POOL_HALO - 1
    in_specs = [
        pl.BlockSpec((None, t, ATTN_WIDTH), lambda bi, i: (bi, i, 0)),
        pl.BlockSpec((None, t, POOL_WIDTH), lambda bi, i: (bi, i, 0)),
        pl.BlockSpec((None, POOL_HALO, POOL_WIDTH),
                     lambda bi, i: (bi, jnp.maximum(i * hb - 1, 0), 0)),
        pl.BlockSpec((None, POOL_HALO, POOL_WIDTH),
                     lambda bi, i: (bi, jnp.minimum((i + 1) * hb, last_hb), 0)),
        pl.BlockSpec((POOL_HALO, POOL_WIDTH), lambda bi, i: (N_META // POOL_HALO - 1, 0)),
        pl.BlockSpec((None, t, GATE_WIDTH), lambda bi, i: (bi, i, 0)),
        pl.BlockSpec((None, t, d), lambda bi, i: (bi, i, 0)),
        _const_spec((ATTN_WIDTH, d)),
        _const_spec((POOL_WIDTH, d)),
        _const_spec((d, d)),
        _const_spec((1, d)),
    ]
    vmem = (2 * t * (ATTN_WIDTH * 2 + POOL_WIDTH * 4 + GATE_WIDTH * 2 + 2 * d * 4)
            + (ATTN_WIDTH + POOL_WIDTH + d) * d * 2 + (t + 2 * POOL_HALO) * POOL_WIDTH * 4
            + 6 * (t // MIX_SPLIT) * d * 4)
    return pl.pallas_call(
        _post_mix_kernel,
        grid=(b, nt),
        in_specs=in_specs,
        out_specs=pl.BlockSpec((None, t, d), lambda bi, i: (bi, i, 0)),
        out_shape=jax.ShapeDtypeStruct((b, s, d), F32),
        scratch_shapes=[pltpu.VMEM((t + 2 * POOL_HALO, POOL_WIDTH), F32)],
        compiler_params=pltpu.CompilerParams(
            dimension_semantics=("arbitrary", "arbitrary"),
            vmem_limit_bytes=_vmem_limit(vmem)),
        name="post_mix",
    )(attn, p_in, p_in, p_in, p_meta, gates, x, w_attn_br, w_pool, w_out, post_g)


def _mlp_kernel(x_ref, g1_ref, w1_ref, w2_ref, g2_ref, o_ref):
    sub = x_ref.shape[0] // MLP_SPLIT
    for part in range(MLP_SPLIT):
        rows = slice(part * sub, (part + 1) * sub)
        o_ref[rows, :] = _mlp_rows(x_ref[rows, :], g1_ref, w1_ref, w2_ref, g2_ref)


def _mlp(x, pre_g, w1, w2, post_g):
    b, s, d = x.shape
    t = MLP_TILE
    sub = t // MLP_SPLIT
    vmem = 4 * t * d * 4 + 2 * d * D_FF * 2 + 4 * sub * FF_CHUNK * 4 + 2 * sub * d * 4
    return pl.pallas_call(
        _mlp_kernel,
        grid=(b, s // t),
        in_specs=[
            pl.BlockSpec((None, t, d), lambda bi, i: (bi, i, 0)),
            _const_spec((1, d)),
            _const_spec((d, D_FF)),
            _const_spec((D_FF, d)),
            _const_spec((1, d)),
        ],
        out_specs=pl.BlockSpec((None, t, d), lambda bi, i: (bi, i, 0)),
        out_shape=jax.ShapeDtypeStruct((b, s, d), F32),
        compiler_params=pltpu.CompilerParams(
            dimension_semantics=("arbitrary", "arbitrary"),
            vmem_limit_bytes=_vmem_limit(vmem)),
        name="mlp",
    )(x, pre_g, w1, w2, post_g)


def _rope_tables(seq):
    inv_freq = ROPE_THETA ** (-np.arange(ROPE_QUARTER, dtype=np.float64) / ROPE_QUARTER)
    pos = np.arange(seq)
    ang_r = (pos // GRID_W)[:, None] * inv_freq[None, :]
    ang_c = (pos % GRID_W)[:, None] * inv_freq[None, :]
    cos = np.concatenate([np.cos(ang_r), np.cos(ang_r), np.cos(ang_c), np.cos(ang_c)], -1)
    sin = np.concatenate([-np.sin(ang_r), np.sin(ang_r), -np.sin(ang_c), np.sin(ang_c)], -1)
    return jnp.asarray(cos, F32), jnp.asarray(sin, F32)


def kernel(x, meta_tokens, pre_mix_g, q_norm_g, k_norm_g, w_in, w_attn_br, w_pool_grp,
           pool_scale, w_pool_br, w_out, post_mix_g, pre_mlp_g, w_mlp_in, w_mlp_out, post_mlp_g):
    b, s, d = x.shape
    assert d == D_MODEL and s % max(IN_TILE, MIX_TILE, MLP_TILE) == 0
    assert (Q_TILE % IN_TILE == 0 or IN_TILE % Q_TILE == 0) and Q_TILE % (IN_TILE // IN_SPLIT) == 0
    assert (IN_TILE // IN_SPLIT) % KV_CHUNK == 0 and s % GRID_W == 0
    assert s % Q_TILE == 0 and KV_UNROLL % 2 == 0 and MIX_TILE % (MIX_SPLIT * BF16_ROWS) == 0
    assert pre_mix_g.shape[0] == 1

    w_in_b = w_in[0].astype(BF16)
    pre_g = pre_mix_g[0][None, :]
    qg = q_norm_g[0][None, :]
    kg = k_norm_g[0][None, :]
    cos_tab, sin_tab = _rope_tables(s)

    (q, k, vt, p_in, gates), (w_attn_b, w_out_b, w_mlp_in_b, w_mlp_out_b) = _in_proj(
        x, pre_g, w_in_b, qg, kg, cos_tab, sin_tab,
        (w_attn_br[0], w_out[0], w_mlp_in[0], w_mlp_out[0]))

    meta_pad = jnp.zeros((META_PAD, d), F32).at[:N_META].set(meta_tokens.astype(F32))
    k_meta, vt_meta, p_meta = _meta_proj(meta_pad, pre_g, w_in_b, kg)

    score_bound = HEAD_DIM * Q_SCALE * jnp.max(jnp.abs(qg)) * jnp.max(jnp.abs(kg))
    attn = lax.cond(
        score_bound <= SCORE_BOUND,
        functools.partial(_attention, bounded=True),
        functools.partial(_attention, bounded=False),
        q, k, vt, k_meta, vt_meta)

    w_pool = _pool_fold(w_pool_grp[0], pool_scale[0][None, :], w_pool_br[0])
    x1 = _post_mix(attn, p_in, p_meta, gates, x, w_attn_b, w_pool, w_out_b,
                   post_mix_g[0][None, :])

    return _mlp(x1, pre_mlp_g[0][None, :], w_mlp_in_b, w_mlp_out_b, post_mlp_g[0][None, :])
```
